```python
import jax, jax.numpy as jnp
from jax import lax
import numpy as np

D_MODEL = 1024
BATCH = 16
SEQ = 2048
DEPTH = 1

D_MIX = 1024
HEAD_DIM = 64
BLOCK = 128
A_HEADS = 8
A_KV_HEADS = 2
A_REP = A_HEADS // A_KV_HEADS
A_WIDTH = A_HEADS * HEAD_DIM
WINDOW = 128
B_HEADS = 8
B_NOPE = 64
B_ROPE = 32
B_V = 64
B_WIDTH = B_HEADS * B_V
Q_LORA = 384
KV_LORA = 256
ROPE_BASE = 10000.0
EPS = 1e-6

IN_SIZES = (
    A_WIDTH,
    A_KV_HEADS * HEAD_DIM,
    A_KV_HEADS * HEAD_DIM,
    A_WIDTH,
    Q_LORA,
    KV_LORA,
    B_ROPE,
    B_WIDTH,
)
IN_COLS = int(sum(IN_SIZES))

kernel_name = "hymba_swa_sink_alibi_mla_encoder"


def rmsnorm(x, gain):
    x32 = x.astype(jnp.float32)
    y = x32 * lax.rsqrt(jnp.mean(x32 * x32, axis=-1, keepdims=True) + EPS)
    return (y * gain.astype(jnp.float32)).astype(x.dtype)


def alibi_slopes(n_heads):
    return jnp.exp2(-8.0 * jnp.arange(1, n_heads + 1, dtype=jnp.float32) / n_heads)


def apply_rope(x, cos, sin):
    x1, x2 = jnp.split(x, 2, axis=-1)
    c = cos[None, :, None, :].astype(x.dtype)
    s = sin[None, :, None, :].astype(x.dtype)
    return jnp.concatenate([x1 * c - x2 * s, x1 * s + x2 * c], axis=-1)


def window_attention(q, k, v, sink):
    B, S = q.shape[0], q.shape[1]
    nb = S // BLOCK
    qb = q.reshape(B, nb, BLOCK, A_KV_HEADS, A_REP, HEAD_DIM)
    pad = ((0, 0), (BLOCK, BLOCK), (0, 0), (0, 0))
    kp = jnp.pad(k, pad).reshape(B, nb + 2, BLOCK, A_KV_HEADS, HEAD_DIM)
    vp = jnp.pad(v, pad).reshape(B, nb + 2, BLOCK, A_KV_HEADS, HEAD_DIM)
    kb = jnp.concatenate([kp[:, :-2], kp[:, 1:-1], kp[:, 2:]], axis=2)
    vb = jnp.concatenate([vp[:, :-2], vp[:, 1:-1], vp[:, 2:]], axis=2)

    scores = jnp.einsum('bnqgrd,bnkgd->bngrqk', qb, kb).astype(jnp.float32)
    scores = scores * (HEAD_DIM ** -0.5)

    qi = jnp.arange(BLOCK)[:, None]
    kj = jnp.arange(3 * BLOCK)[None, :]
    dist = jnp.abs(qi - kj + BLOCK).astype(jnp.float32)
    s_pos = (jnp.arange(nb)[:, None] * BLOCK - BLOCK + jnp.arange(3 * BLOCK)[None, :])
    valid_s = (s_pos >= 0) & (s_pos < S)
    mask = (dist <= WINDOW)[None] & valid_s[:, None, :]
    slopes = alibi_slopes(A_HEADS).reshape(A_KV_HEADS, A_REP)
    bias = -slopes[:, :, None, None] * dist[None, None]
    scores = jnp.where(mask[None, :, None, None], scores + bias[None, None], -jnp.inf)

    sink_l = sink.astype(jnp.float32).reshape(A_KV_HEADS, A_REP)[None, None, :, :, None, None]
    m = jnp.maximum(jnp.max(scores, axis=-1, keepdims=True), sink_l)
    p = jnp.exp(scores - m)
    denom = jnp.sum(p, axis=-1, keepdims=True) + jnp.exp(sink_l - m)
    probs = (p / denom).astype(v.dtype)
    out = jnp.einsum('bngrqk,bnkgd->bnqgrd', probs, vb)
    return out.reshape(B, S, A_WIDTH)


def latent_attention(q_nope, q_rope, k_nope, k_rope, v):
    B, S = q_nope.shape[0], q_nope.shape[1]
    nb = S // BLOCK
    scale = (B_NOPE + B_ROPE) ** -0.5
    qn = q_nope.reshape(B, nb, BLOCK, B_HEADS, B_NOPE).transpose(1, 0, 2, 3, 4)
    qr = q_rope.reshape(B, nb, BLOCK, B_HEADS, B_ROPE).transpose(1, 0, 2, 3, 4)

    def one_block(args):
        qn_b, qr_b = args
        s = (jnp.einsum('bqhd,bkhd->bhqk', qn_b, k_nope)
             + jnp.einsum('bqhr,bkr->bhqk', qr_b, k_rope)).astype(jnp.float32) * scale
        p = jax.nn.softmax(s, axis=-1).astype(v.dtype)
        return jnp.einsum('bhqk,bkhd->bqhd', p, v)

    out = lax.map(one_block, (qn, qr))
    return out.transpose(1, 0, 2, 3, 4).reshape(B, S, B_WIDTH)


def setup_inputs(seed: int = 0) -> dict:
    key = jax.random.key(seed)
    ks = jax.random.split(key, 12)
    f32 = jnp.float32
    x = jax.random.normal(ks[0], (BATCH, SEQ, D_MODEL), f32)
    norm_mix = 1.0 + 0.01 * jax.random.normal(ks[1], (DEPTH, D_MODEL), f32)
    w_in = jax.random.normal(ks[2], (DEPTH, D_MODEL, IN_COLS), f32) * D_MODEL ** -0.5
    attn_sink = 0.5 * jax.random.normal(ks[3], (DEPTH, A_HEADS), f32)
    q_a_norm = 1.0 + 0.01 * jax.random.normal(ks[4], (DEPTH, Q_LORA), f32)
    w_uq = jax.random.normal(ks[5], (DEPTH, Q_LORA, B_HEADS * (B_NOPE + B_ROPE)), f32) * Q_LORA ** -0.5
    kv_a_norm = 1.0 + 0.01 * jax.random.normal(ks[6], (DEPTH, KV_LORA), f32)
    w_ukv = jax.random.normal(ks[7], (DEPTH, KV_LORA, B_HEADS * (B_NOPE + B_V)), f32) * KV_LORA ** -0.5
    w_out = jax.random.normal(ks[8], (DEPTH, D_MIX, D_MODEL), f32) * D_MIX ** -0.5
    final_norm = 1.0 + 0.01 * jax.random.normal(ks[9], (D_MODEL,), f32)
    return {"x": x, "norm_mix": norm_mix, "w_in": w_in, "attn_sink": attn_sink,
            "q_a_norm": q_a_norm, "w_uq": w_uq, "kv_a_norm": kv_a_norm, "w_ukv": w_ukv,
            "w_out": w_out, "final_norm": final_norm}


def reference(x, norm_mix, w_in, attn_sink, q_a_norm, w_uq, kv_a_norm, w_ukv, w_out, final_norm):
    B, S = x.shape[0], x.shape[1]
    split_idx = np.cumsum(np.array(IN_SIZES))[:-1].tolist()
    pos = jnp.arange(S, dtype=jnp.float32)
    inv_freq = ROPE_BASE ** (-jnp.arange(0, B_ROPE, 2, dtype=jnp.float32) / B_ROPE)
    ang = pos[:, None] * inv_freq[None, :]
    cos, sin = jnp.cos(ang), jnp.sin(ang)

    for l in range(DEPTH):
        h = rmsnorm(x, norm_mix[l])
        proj = jnp.einsum('bsd,dc->bsc', h, w_in[l])
        qa, ka, va, ga, cq, ckv, kr, gb = jnp.split(proj, split_idx, axis=-1)

        ya = window_attention(qa.reshape(B, S, A_HEADS, HEAD_DIM),
                              ka.reshape(B, S, A_KV_HEADS, HEAD_DIM),
                              va.reshape(B, S, A_KV_HEADS, HEAD_DIM),
                              attn_sink[l])
        ya = ya * jax.nn.silu(ga)

        q = jnp.einsum('bsc,ce->bse', rmsnorm(cq, q_a_norm[l]), w_uq[l])
        q = q.reshape(B, S, B_HEADS, B_NOPE + B_ROPE)
        q_nope, q_rope = q[..., :B_NOPE], apply_rope(q[..., B_NOPE:], cos, sin)
        kv = jnp.einsum('bsc,ce->bse', rmsnorm(ckv, kv_a_norm[l]), w_ukv[l])
        kv = kv.reshape(B, S, B_HEADS, B_NOPE + B_V)
        k_nope, v_b = kv[..., :B_NOPE], kv[..., B_NOPE:]
        k_rope = apply_rope(kr.reshape(B, S, 1, B_ROPE), cos, sin)[:, :, 0, :]
        yb = latent_attention(q_nope, q_rope, k_nope, k_rope, v_b)
        yb = yb * jax.nn.silu(gb)

        y = jnp.concatenate([ya, yb], axis=-1)
        x = x + jnp.einsum('bsc,cd->bsd', y, w_out[l])

    return rmsnorm(x, final_norm)
```

```python
import functools

import jax
import jax.numpy as jnp
import numpy as np
from jax.experimental import pallas as pl
from jax.experimental.pallas import tpu as pltpu

D_MODEL = 1024
HEAD_DIM = 64
BLOCK = 128
A_HEADS = 8
A_KV_HEADS = 2
A_REP = A_HEADS // A_KV_HEADS
A_WIDTH = A_HEADS * HEAD_DIM
WINDOW = 128
B_HEADS = 8
B_NOPE = 64
B_ROPE = 32
B_V = 64
B_WIDTH = B_HEADS * B_V
Q_LORA = 384
KV_LORA = 256
ROPE_BASE = 10000.0
EPS = 1e-6

LANES = 128
HALF_ROPE = B_ROPE // 2
NEG = -1e30
VMEM_LIMIT = 56 * 1024 * 1024

NOPE_A = 48

C_QA = 0
C_KK = C_QA + A_WIDTH
C_VV = C_KK + A_KV_HEADS * LANES
C_GATE = C_VV + A_KV_HEADS * LANES
C_CQ = C_GATE + A_WIDTH + B_WIDTH
C_CKV = C_CQ + Q_LORA
C_KR = C_CKV + KV_LORA
C_END = C_KR + LANES

TM_PROJ = 512
TQ_LAT = 256
TM_OUT = 512


def _rms(x, gain):
    return x * jax.lax.rsqrt(jnp.mean(x * x, axis=-1, keepdims=True) + EPS) * gain


def _rope(slab, cos, sin):
    return slab * cos + pltpu.roll(slab, 64, 1) * sin


def _proj_kernel(x_ref, g_ref, win_ref, qg_ref, wuq_ref, kvg_ref, wk_ref, wv_ref,
                 cq_ref, sq_ref, ck_ref, sk_ref,
                 qa_ref, kkvv_ref, gate_ref, qb_ref, kb_ref, vb_ref):
    h = _rms(x_ref[...], g_ref[...]).astype(jnp.bfloat16)

    def seg(lo, hi):
        return jnp.dot(h, win_ref[:, lo:hi], preferred_element_type=jnp.float32)

    qa_ref[...] = seg(C_QA, C_KK).astype(jnp.bfloat16)
    kkvv_ref[...] = seg(C_KK, C_GATE).astype(jnp.bfloat16)
    g = seg(C_GATE, C_CQ)
    gate_ref[...] = (g / (1.0 + jnp.exp(-g))).astype(jnp.bfloat16)

    cqn = _rms(seg(C_CQ, C_CKV), qg_ref[...]).astype(jnp.bfloat16)
    q = jnp.dot(cqn, wuq_ref[...], preferred_element_type=jnp.float32)
    cq, sq = cq_ref[...], sq_ref[...]
    for hd in range(B_HEADS):
        sl = slice(hd * LANES, (hd + 1) * LANES)
        qb_ref[:, sl] = _rope(q[:, sl], cq, sq).astype(jnp.bfloat16)

    ckvn = _rms(seg(C_CKV, C_KR), kvg_ref[...]).astype(jnp.bfloat16)
    kr = _rope(seg(C_KR, C_END), ck_ref[...], sk_ref[...])
    kn = jnp.dot(ckvn, wk_ref[...], preferred_element_type=jnp.float32)
    for hd in range(B_HEADS):
        sl = slice(hd * LANES, (hd + 1) * LANES)
        kb_ref[:, sl] = (kn[:, sl] + kr).astype(jnp.bfloat16)
    vb_ref[...] = jnp.dot(ckvn, wv_ref[...], preferred_element_type=jnp.float32).astype(jnp.bfloat16)


def _input_proj(x2, norm_g, w_in_p, q_g, w_uq_p, kv_g, w_k_p, w_v, cq, sq, ck, sk, seq):
    t = x2.shape[0]
    tm = TM_PROJ
    n_pos = seq // tm
    full = lambda a: pl.BlockSpec(a.shape, lambda i: (0,) * a.ndim)
    rows = lambda w: pl.BlockSpec((tm, w), lambda i: (i, 0))
    tab = pl.BlockSpec((tm, LANES), lambda i: (i % n_pos, 0))
    bf = jnp.bfloat16
    return pl.pallas_call(
        _proj_kernel,
        grid=(t // tm,),
        in_specs=[rows(D_MODEL), full(norm_g), full(w_in_p), full(q_g), full(w_uq_p),
                  full(kv_g), full(w_k_p), full(w_v), tab, tab, tab, tab],
        out_specs=[rows(A_WIDTH), rows(2 * A_KV_HEADS * LANES), rows(A_WIDTH + B_WIDTH),
                   rows(B_HEADS * LANES), rows(B_HEADS * LANES), rows(B_WIDTH)],
        out_shape=[jax.ShapeDtypeStruct((t, A_WIDTH), bf),
                   jax.ShapeDtypeStruct((t, 2 * A_KV_HEADS * LANES), bf),
                   jax.ShapeDtypeStruct((t, A_WIDTH + B_WIDTH), bf),
                   jax.ShapeDtypeStruct((t, B_HEADS * LANES), bf),
                   jax.ShapeDtypeStruct((t, B_HEADS * LANES), bf),
                   jax.ShapeDtypeStruct((t, B_WIDTH), bf)],
        compiler_params=pltpu.CompilerParams(
            dimension_semantics=("arbitrary",), vmem_limit_bytes=VMEM_LIMIT),
        name="input_proj",
    )(x2, norm_g, w_in_p, q_g, w_uq_p, kv_g, w_k_p, w_v, cq, sq, ck, sk)


def _window_kernel(q_ref, kkvv_ref, gate_ref, bias_ref, sink_ref, o_ref, *, n_blocks):
    n = pl.program_id(1)
    lo = pl.multiple_of(jnp.maximum(n - 1, 0) * BLOCK, BLOCK)
    hi = pl.multiple_of(jnp.minimum(n + 1, n_blocks - 1) * BLOCK, BLOCK)
    mid = pl.multiple_of(n * BLOCK, BLOCK)
    band = jnp.concatenate([kkvv_ref[pl.ds(lo, BLOCK), :],
                            kkvv_ref[pl.ds(mid, BLOCK), :],
                            kkvv_ref[pl.ds(hi, BLOCK), :]], axis=0)
    lane = jax.lax.broadcasted_iota(jnp.int32, (BLOCK, LANES), 1)
    first = lane < HEAD_DIM
    col = jax.lax.broadcasted_iota(jnp.int32, (1, 3 * BLOCK), 1)
    edge_ok = ((col >= BLOCK) | (n > 0)) & ((col < 2 * BLOCK) | (n < n_blocks - 1))
    zero = jnp.zeros((), jnp.bfloat16)

    for g in range(A_KV_HEADS):
        kk = band[:, g * LANES:(g + 1) * LANES]
        vv = band[:, (A_KV_HEADS + g) * LANES:(A_KV_HEADS + g + 1) * LANES]
        qs = []
        for j in range(A_REP // 2):
            qp = q_ref[:, (g * 2 + j) * LANES:(g * 2 + j + 1) * LANES]
            qs += [jnp.where(first, qp, zero), jnp.where(first, zero, qp)]
        q4 = jnp.concatenate(qs, axis=0)
        s = jax.lax.dot_general(q4, kk, (((1,), (1,)), ((), ())),
                                preferred_element_type=jnp.float32)
        s = jnp.where(edge_ok, s + bias_ref[g * A_REP * BLOCK:(g + 1) * A_REP * BLOCK, :], NEG)
        sink = sink_ref[g * A_REP * BLOCK:(g + 1) * A_REP * BLOCK, :]
        m = jnp.maximum(jnp.max(s, axis=-1, keepdims=True), sink)
        p = jnp.exp(s - m)
        inv = 1.0 / (jnp.sum(p, axis=-1, keepdims=True) + jnp.exp(sink - m))
        pb = p.astype(jnp.bfloat16)
        for j in range(A_REP // 2):
            r0 = (2 * j) * BLOCK
            r1 = (2 * j + 1) * BLOCK
            o0 = jnp.dot(pb[r0:r0 + BLOCK], vv, preferred_element_type=jnp.float32)
            o1 = jnp.dot(pb[r1:r1 + BLOCK], vv, preferred_element_type=jnp.float32)
            o = jnp.where(first, o0 * inv[r0:r0 + BLOCK], o1 * inv[r1:r1 + BLOCK])
            sl = slice((g * 2 + j) * LANES, (g * 2 + j + 1) * LANES)
            o_ref[:, sl] = (o * gate_ref[:, sl].astype(jnp.float32)).astype(jnp.bfloat16)


def _window_attn(qa, kkvv, gate, bias, sink_rows, batch, seq):
    nb = seq // BLOCK
    return pl.pallas_call(
        functools.partial(_window_kernel, n_blocks=nb),
        grid=(batch, nb),
        in_specs=[pl.BlockSpec((BLOCK, A_WIDTH), lambda b, n: (b * nb + n, 0)),
                  pl.BlockSpec((seq, 2 * A_KV_HEADS * LANES), lambda b, n: (b, 0)),
                  pl.BlockSpec((BLOCK, A_WIDTH), lambda b, n: (b * nb + n, 0)),
                  pl.BlockSpec(bias.shape, lambda b, n: (0, 0)),
                  pl.BlockSpec(sink_rows.shape, lambda b, n: (0, 0))],
        out_specs=pl.BlockSpec((BLOCK, A_WIDTH), lambda b, n: (b * nb + n, 0)),
        out_shape=jax.ShapeDtypeStruct((batch * seq, A_WIDTH), jnp.bfloat16),
        compiler_params=pltpu.CompilerParams(
            dimension_semantics=("arbitrary", "arbitrary"), vmem_limit_bytes=VMEM_LIMIT),
        name="window_attn",
    )(qa, kkvv, gate, bias, sink_rows)


def _latent_kernel(q_ref, k_ref, v_ref, gate_ref, o_ref):
    lane = jax.lax.broadcasted_iota(jnp.int32, (q_ref.shape[0], LANES), 1)
    first = lane < B_V
    for j in range(B_HEADS // 2):
        vv = v_ref[:, j * LANES:(j + 1) * LANES]
        outs = []
        for hd in (2 * j, 2 * j + 1):
            sl = slice(hd * LANES, (hd + 1) * LANES)
            s = jax.lax.dot_general(q_ref[:, sl], k_ref[:, sl], (((1,), (1,)), ((), ())),
                                    preferred_element_type=jnp.float32)
            m = jnp.max(s, axis=-1, keepdims=True)
            p = jnp.exp(s - m)
            inv = 1.0 / jnp.sum(p, axis=-1, keepdims=True)
            o = jnp.dot(p.astype(jnp.bfloat16), vv, preferred_element_type=jnp.float32)
            outs.append(o * inv)
        o = jnp.where(first, outs[0], outs[1])
        sl = slice(j * LANES, (j + 1) * LANES)
        o_ref[:, sl] = (o * gate_ref[:, sl].astype(jnp.float32)).astype(jnp.bfloat16)


def _latent_attn(qb, kb, vb, gate, batch, seq):
    tq = TQ_LAT
    nq = seq // tq
    return pl.pallas_call(
        _latent_kernel,
        grid=(batch, nq),
        in_specs=[pl.BlockSpec((tq, B_HEADS * LANES), lambda b, i: (b * nq + i, 0)),
                  pl.BlockSpec((seq, B_HEADS * LANES), lambda b, i: (b, 0)),
                  pl.BlockSpec((seq, B_WIDTH), lambda b, i: (b, 0)),
                  pl.BlockSpec((tq, B_WIDTH), lambda b, i: (b * nq + i, 1))],
        out_specs=pl.BlockSpec((tq, B_WIDTH), lambda b, i: (b * nq + i, 0)),
        out_shape=jax.ShapeDtypeStruct((batch * seq, B_WIDTH), jnp.bfloat16),
        compiler_params=pltpu.CompilerParams(
            dimension_semantics=("arbitrary", "arbitrary"), vmem_limit_bytes=VMEM_LIMIT),
        name="latent_attn",
    )(qb, kb, vb, gate)


def _out_kernel(ya_ref, yb_ref, x_ref, wa_ref, wb_ref, g_ref, o_ref):
    d = jnp.dot(ya_ref[...], wa_ref[...], preferred_element_type=jnp.float32)
    d = d + jnp.dot(yb_ref[...], wb_ref[...], preferred_element_type=jnp.float32)
    o_ref[...] = _rms(x_ref[...] + d, g_ref[...])


def _output_proj(ya, yb, x2, w_a, w_b, final_g):
    t = x2.shape[0]
    tm = TM_OUT
    full = lambda a: pl.BlockSpec(a.shape, lambda i: (0,) * a.ndim)
    rows = lambda w: pl.BlockSpec((tm, w), lambda i: (i, 0))
    return pl.pallas_call(
        _out_kernel,
        grid=(t // tm,),
        in_specs=[rows(A_WIDTH), rows(B_WIDTH), rows(D_MODEL), full(w_a), full(w_b), full(final_g)],
        out_specs=rows(D_MODEL),
        out_shape=jax.ShapeDtypeStruct((t, D_MODEL), jnp.float32),
        compiler_params=pltpu.CompilerParams(
            dimension_semantics=("arbitrary",), vmem_limit_bytes=VMEM_LIMIT),
        name="output_proj",
    )(ya, yb, x2, w_a, w_b, final_g)


def _head_slab(x1, nope, x2):
    z = jnp.zeros((nope.shape[0], LANES - B_NOPE - B_ROPE), nope.dtype)
    return jnp.concatenate([x1, nope[:, :NOPE_A], x2, nope[:, NOPE_A:], z], axis=1)


def _pack_weights(w_in, w_uq, w_ukv):
    sizes = (A_WIDTH, A_KV_HEADS * HEAD_DIM, A_KV_HEADS * HEAD_DIM, A_WIDTH,
             Q_LORA, KV_LORA, B_ROPE, B_WIDTH)
    offs = np.cumsum((0,) + sizes)
    qa, ka, va, ga, cq, ckv, kr, gb = (w_in[:, offs[i]:offs[i + 1]] for i in range(8))
    dup = lambda w: jnp.concatenate(
        [w[:, g * HEAD_DIM:(g + 1) * HEAD_DIM] for g in range(A_KV_HEADS) for _ in range(2)], axis=1)
    zr = jnp.zeros((w_in.shape[0], B_NOPE), w_in.dtype)
    kr_slab = _head_slab(kr[:, :HALF_ROPE], zr, kr[:, HALF_ROPE:])
    w_in_p = jnp.concatenate(
        [qa * (HEAD_DIM ** -0.5), dup(ka), dup(va), ga, gb, cq, ckv, kr_slab], axis=1)

    per_q = B_NOPE + B_ROPE
    uq_slabs, k_slabs, v_cols = [], [], []
    zq = jnp.zeros((KV_LORA, HALF_ROPE), w_ukv.dtype)
    for hd in range(B_HEADS):
        wq = w_uq[:, hd * per_q:(hd + 1) * per_q]
        uq_slabs.append(_head_slab(wq[:, B_NOPE:B_NOPE + HALF_ROPE], wq[:, :B_NOPE],
                                   wq[:, B_NOPE + HALF_ROPE:]))
        wkv = w_ukv[:, hd * (B_NOPE + B_V):(hd + 1) * (B_NOPE + B_V)]
        k_slabs.append(_head_slab(zq, wkv[:, :B_NOPE], zq))
        v_cols.append(wkv[:, B_NOPE:])
    bf = jnp.bfloat16
    return (w_in_p.astype(bf), jnp.concatenate(uq_slabs, axis=1).astype(bf),
            jnp.concatenate(k_slabs, axis=1).astype(bf), jnp.concatenate(v_cols, axis=1).astype(bf))


def _rope_tables(seq):
    pos = jnp.arange(seq, dtype=jnp.float32)
    inv_freq = ROPE_BASE ** (-jnp.arange(0, B_ROPE, 2, dtype=jnp.float32) / B_ROPE)
    ang = pos[:, None] * inv_freq[None, :]
    cos, sin = jnp.cos(ang), jnp.sin(ang)
    one = jnp.ones((seq, B_NOPE), jnp.float32)
    zero = jnp.zeros((seq, B_NOPE), jnp.float32)
    c = _head_slab(cos, one, cos)
    s = _head_slab(-sin, zero, sin)
    scale = (B_NOPE + B_ROPE) ** -0.5
    return c * scale, s * scale, c, s


def _window_bias():
    qi = np.arange(BLOCK)[:, None]
    kj = np.arange(3 * BLOCK)[None, :]
    dist = np.abs(qi - kj + BLOCK).astype(np.float32)
    slopes = np.exp2(-8.0 * np.arange(1, A_HEADS + 1, dtype=np.float32) / A_HEADS)
    bias = -slopes[:, None, None] * dist[None]
    bias = np.where((dist <= WINDOW)[None], bias, NEG).astype(np.float32)
    return jnp.asarray(bias.reshape(A_HEADS * BLOCK, 3 * BLOCK))


def kernel(x, norm_mix, w_in, attn_sink, q_a_norm, w_uq, kv_a_norm, w_ukv, w_out, final_norm):
    batch, seq, _ = x.shape
    assert norm_mix.shape[0] == 1, "single-layer block: the final norm is fused into the layer"
    assert seq % TM_PROJ == 0 and seq % TQ_LAT == 0 and (batch * seq) % TM_OUT == 0
    cq, sq, ck, sk = _rope_tables(seq)
    bias = _window_bias()
    x2 = x.reshape(batch * seq, D_MODEL)
    w_in_p, w_uq_p, w_k_p, w_v = _pack_weights(w_in[0], w_uq[0], w_ukv[0])
    qa, kkvv, gate, qb, kb, vb = _input_proj(
        x2, norm_mix, w_in_p, q_a_norm, w_uq_p, kv_a_norm, w_k_p, w_v, cq, sq, ck, sk, seq)
    sink_rows = jnp.repeat(attn_sink[0].astype(jnp.float32), BLOCK)[:, None]
    ya = _window_attn(qa, kkvv, gate, bias, sink_rows, batch, seq)
    yb = _latent_attn(qb, kb, vb, gate, batch, seq)
    w_o = w_out[0].astype(jnp.bfloat16)
    out = _output_proj(ya, yb, x2, w_o[:A_WIDTH], w_o[A_WIDTH:], final_norm[None])
    return out.reshape(batch, seq, D_MODEL)
```

```python
import functools

import jax
import jax.numpy as jnp
import numpy as np
from jax.experimental import pallas as pl
from jax.experimental.pallas import tpu as pltpu

D_MODEL = 1024
HEAD_DIM = 64
BLOCK = 128
A_HEADS = 8
A_KV_HEADS = 2
A_REP = A_HEADS // A_KV_HEADS
A_WIDTH = A_HEADS * HEAD_DIM
WINDOW = 128
B_HEADS = 8
B_NOPE = 64
B_ROPE = 32
B_V = 64
B_WIDTH = B_HEADS * B_V
Q_LORA = 384
KV_LORA = 256
ROPE_BASE = 10000.0
EPS = 1e-6

LANES = 128
HALF_ROPE = B_ROPE // 2
NEG = -1e30
LOG2E = float(np.log2(np.e))
QA_SCALE = HEAD_DIM ** -0.5 * LOG2E
QB_SCALE = (B_NOPE + B_ROPE) ** -0.5 * LOG2E
VMEM_LIMIT = 56 * 1024 * 1024

NOPE_A = 48

A_PAIR_COLS = np.concatenate(
    [np.arange(HEAD_DIM) + (g * A_REP + p) * HEAD_DIM for p in range(A_REP) for g in range(A_KV_HEADS)])

C_QA = 0
C_KV = C_QA + A_WIDTH
C_GATE = C_KV + 2 * A_KV_HEADS * HEAD_DIM
C_CQ = C_GATE + A_WIDTH + B_WIDTH
C_CKV = C_CQ + Q_LORA
C_KR = C_CKV + KV_LORA
C_END = C_KR + LANES

TM_PROJ = 512
QB_WIN = 4
TQ_LAT = 256
TM_OUT = 512


def _rms(x, gain):
    return x * jax.lax.rsqrt(jnp.mean(x * x, axis=-1, keepdims=True) + EPS) * gain


def _rope(slab, cos, sin):
    return slab * cos + pltpu.roll(slab, 64, 1) * sin


def _proj_kernel(x_ref, g_ref, win_ref, qg_ref, wuq_ref, kvg_ref, wk_ref, wv_ref,
                 cq_ref, sq_ref, ck_ref, sk_ref,
                 qa_ref, kva_ref, gate_ref, qb_ref, kb_ref, vb_ref):
    h = _rms(x_ref[...], g_ref[...]).astype(jnp.bfloat16)

    def seg(lo, hi):
        return jnp.dot(h, win_ref[:, lo:hi], preferred_element_type=jnp.float32)

    qa_ref[...] = (seg(C_QA, C_KV) * QA_SCALE).astype(jnp.bfloat16)
    kva_ref[...] = seg(C_KV, C_GATE).astype(jnp.bfloat16)
    g = seg(C_GATE, C_CQ)
    gate_ref[...] = (g / (1.0 + jnp.exp(-g))).astype(jnp.bfloat16)

    cqn = _rms(seg(C_CQ, C_CKV), qg_ref[...]).astype(jnp.bfloat16)
    q = jnp.dot(cqn, wuq_ref[...], preferred_element_type=jnp.float32)
    cq, sq = cq_ref[...], sq_ref[...]
    for hd in range(B_HEADS):
        sl = slice(hd * LANES, (hd + 1) * LANES)
        qb_ref[:, sl] = _rope(q[:, sl], cq, sq).astype(jnp.bfloat16)

    ckvn = _rms(seg(C_CKV, C_KR), kvg_ref[...]).astype(jnp.bfloat16)
    kr = _rope(seg(C_KR, C_END), ck_ref[...], sk_ref[...])
    kn = jnp.dot(ckvn, wk_ref[...], preferred_element_type=jnp.float32)
    for hd in range(B_HEADS):
        sl = slice(hd * LANES, (hd + 1) * LANES)
        kb_ref[:, sl] = (kn[:, sl] + kr).astype(jnp.bfloat16)
    vb_ref[...] = jnp.dot(ckvn, wv_ref[...], preferred_element_type=jnp.float32).astype(jnp.bfloat16)


def _input_proj(x2, norm_g, w_in_p, q_g, w_uq_p, kv_g, w_k_p, w_v, cq, sq, ck, sk, seq):
    t = x2.shape[0]
    tm = TM_PROJ
    n_pos = seq // tm
    full = lambda a: pl.BlockSpec(a.shape, lambda i: (0,) * a.ndim)
    rows = lambda w: pl.BlockSpec((tm, w), lambda i: (i, 0))
    tab = pl.BlockSpec((tm, LANES), lambda i: (i % n_pos, 0))
    bf = jnp.bfloat16
    widths = (A_WIDTH, 2 * A_KV_HEADS * HEAD_DIM, A_WIDTH + B_WIDTH,
              B_HEADS * LANES, B_HEADS * LANES, B_WIDTH)
    return pl.pallas_call(
        _proj_kernel,
        grid=(t // tm,),
        in_specs=[rows(D_MODEL), full(norm_g), full(w_in_p), full(q_g), full(w_uq_p),
                  full(kv_g), full(w_k_p), full(w_v), tab, tab, tab, tab],
        out_specs=[rows(w) for w in widths],
        out_shape=[jax.ShapeDtypeStruct((t, w), bf) for w in widths],
        compiler_params=pltpu.CompilerParams(
            dimension_semantics=("arbitrary",), vmem_limit_bytes=VMEM_LIMIT),
        name="input_proj",
    )(x2, norm_g, w_in_p, q_g, w_uq_p, kv_g, w_k_p, w_v, cq, sq, ck, sk)


def _window_block(q_ref, kva_ref, vt_ref, gate_ref, bias_ref, sink_ref, o_ref,
                  rows, n, bias_idx, n_blocks):
    band = (jnp.maximum(n - 1, 0), n, jnp.minimum(n + 1, n_blocks - 1))
    kk = jnp.concatenate(
        [kva_ref[pl.ds(pl.multiple_of(b * BLOCK, BLOCK), BLOCK), :LANES] for b in band],
        axis=0)
    vts = [vt_ref[b] for b in band]
    first = jax.lax.broadcasted_iota(jnp.int32, (BLOCK, LANES), 1) < HEAD_DIM
    zero = jnp.zeros((), jnp.bfloat16)
    qp = [q_ref[rows, p * LANES:(p + 1) * LANES] for p in range(A_REP)]

    outs = []
    for g in range(A_KV_HEADS):
        keep = first if g == 0 else jnp.logical_not(first)
        q4 = jnp.concatenate([jnp.where(keep, x, zero) for x in qp], axis=0)
        st = jax.lax.dot_general(kk, q4, (((1,), (1,)), ((), ())),
                                 preferred_element_type=jnp.float32)
        st = st + bias_ref[bias_idx, g]
        sink = sink_ref[g:g + 1, :]
        m = jnp.maximum(jnp.max(st, axis=0, keepdims=True), sink)
        p = jnp.exp2(st - m)
        inv = 1.0 / (jnp.sum(p, axis=0, keepdims=True) + jnp.exp2(sink - m))
        vt = jnp.concatenate([x[g * HEAD_DIM:(g + 1) * HEAD_DIM, :] for x in vts], axis=1)
        ot = jnp.dot(vt, p.astype(jnp.bfloat16), preferred_element_type=jnp.float32)
        outs.append(ot * inv)

    for pr in range(A_REP):
        sl = slice(pr * LANES, (pr + 1) * LANES)
        o = jnp.concatenate([outs[0][:, sl], outs[1][:, sl]], axis=0).T
        o_ref[rows, sl] = (o * gate_ref[rows, sl].astype(jnp.float32)).astype(jnp.bfloat16)


def _window_kernel(q_ref, kva_ref, gate_ref, bias_ref, sink_ref, o_ref, vt_ref, *, n_blocks):
    step = pl.program_id(1)
    n_steps = n_blocks // QB_WIN

    @pl.when(step == 0)
    def _():
        for b in range(n_blocks):
            v = kva_ref[b * BLOCK:(b + 1) * BLOCK, LANES:].astype(jnp.float32)
            vt_ref[b] = v.T.astype(jnp.bfloat16)

    for blk in range(QB_WIN):
        if blk == 0:
            bias_idx = jnp.where(step == 0, 0, 1)
        elif blk == QB_WIN - 1:
            bias_idx = jnp.where(step == n_steps - 1, 2, 1)
        else:
            bias_idx = 1
        _window_block(q_ref, kva_ref, vt_ref, gate_ref, bias_ref, sink_ref, o_ref,
                      pl.ds(blk * BLOCK, BLOCK), step * QB_WIN + blk, bias_idx, n_blocks)


def _window_attn(qa, kva, gate, bias, sink_tab, batch, seq):
    nb = seq // BLOCK
    ns = nb // QB_WIN
    tq = QB_WIN * BLOCK
    return pl.pallas_call(
        functools.partial(_window_kernel, n_blocks=nb),
        grid=(batch, ns),
        in_specs=[pl.BlockSpec((tq, A_WIDTH), lambda b, n: (b * ns + n, 0)),
                  pl.BlockSpec((seq, 2 * A_KV_HEADS * HEAD_DIM), lambda b, n: (b, 0)),
                  pl.BlockSpec((tq, A_WIDTH), lambda b, n: (b * ns + n, 0)),
                  pl.BlockSpec(bias.shape, lambda b, n: (0, 0, 0, 0)),
                  pl.BlockSpec(sink_tab.shape, lambda b, n: (0, 0))],
        out_specs=pl.BlockSpec((tq, A_WIDTH), lambda b, n: (b * ns + n, 0)),
        out_shape=jax.ShapeDtypeStruct((batch * seq, A_WIDTH), jnp.bfloat16),
        scratch_shapes=[pltpu.VMEM((nb, LANES, BLOCK), jnp.bfloat16)],
        compiler_params=pltpu.CompilerParams(
            dimension_semantics=("arbitrary", "arbitrary"), vmem_limit_bytes=VMEM_LIMIT),
        name="window_attn",
    )(qa, kva, gate, bias, sink_tab)


def _latent_kernel(q_ref, k_ref, v_ref, gate_ref, o_ref):
    lane = jax.lax.broadcasted_iota(jnp.int32, (q_ref.shape[0], LANES), 1)
    first = lane < B_V
    for j in range(B_HEADS // 2):
        vv = v_ref[:, j * LANES:(j + 1) * LANES]
        outs = []
        for hd in (2 * j, 2 * j + 1):
            sl = slice(hd * LANES, (hd + 1) * LANES)
            s = jax.lax.dot_general(q_ref[:, sl], k_ref[:, sl], (((1,), (1,)), ((), ())),
                                    preferred_element_type=jnp.float32)
            m = jnp.max(s, axis=-1, keepdims=True)
            p = jnp.exp2(s - m)
            inv = 1.0 / jnp.sum(p, axis=-1, keepdims=True)
            o = jnp.dot(p.astype(jnp.bfloat16), vv, preferred_element_type=jnp.float32)
            outs.append(o * inv)
        o = jnp.where(first, outs[0], outs[1])
        sl = slice(j * LANES, (j + 1) * LANES)
        o_ref[:, sl] = (o * gate_ref[:, sl].astype(jnp.float32)).astype(jnp.bfloat16)


def _latent_attn(qb, kb, vb, gate, batch, seq):
    tq = TQ_LAT
    nq = seq // tq
    return pl.pallas_call(
        _latent_kernel,
        grid=(batch, nq),
        in_specs=[pl.BlockSpec((tq, B_HEADS * LANES), lambda b, i: (b * nq + i, 0)),
                  pl.BlockSpec((seq, B_HEADS * LANES), lambda b, i: (b, 0)),
                  pl.BlockSpec((seq, B_WIDTH), lambda b, i: (b, 0)),
                  pl.BlockSpec((tq, B_WIDTH), lambda b, i: (b * nq + i, 1))],
        out_specs=pl.BlockSpec((tq, B_WIDTH), lambda b, i: (b * nq + i, 0)),
        out_shape=jax.ShapeDtypeStruct((batch * seq, B_WIDTH), jnp.bfloat16),
        compiler_params=pltpu.CompilerParams(
            dimension_semantics=("arbitrary", "arbitrary"), vmem_limit_bytes=VMEM_LIMIT),
        name="latent_attn",
    )(qb, kb, vb, gate)


def _out_kernel(ya_ref, yb_ref, x_ref, wa_ref, wb_ref, g_ref, o_ref):
    d = jnp.dot(ya_ref[...], wa_ref[...], preferred_element_type=jnp.float32)
    d = d + jnp.dot(yb_ref[...], wb_ref[...], preferred_element_type=jnp.float32)
    o_ref[...] = _rms(x_ref[...] + d, g_ref[...])


def _output_proj(ya, yb, x2, w_a, w_b, final_g):
    t = x2.shape[0]
    tm = TM_OUT
    full = lambda a: pl.BlockSpec(a.shape, lambda i: (0,) * a.ndim)
    rows = lambda w: pl.BlockSpec((tm, w), lambda i: (i, 0))
    return pl.pallas_call(
        _out_kernel,
        grid=(t // tm,),
        in_specs=[rows(A_WIDTH), rows(B_WIDTH), rows(D_MODEL), full(w_a), full(w_b), full(final_g)],
        out_specs=rows(D_MODEL),
        out_shape=jax.ShapeDtypeStruct((t, D_MODEL), jnp.float32),
        compiler_params=pltpu.CompilerParams(
            dimension_semantics=("arbitrary",), vmem_limit_bytes=VMEM_LIMIT),
        name="output_proj",
    )(ya, yb, x2, w_a, w_b, final_g)


def _head_slab(x1, nope, x2):
    z = jnp.zeros((nope.shape[0], LANES - B_NOPE - B_ROPE), nope.dtype)
    return jnp.concatenate([x1, nope[:, :NOPE_A], x2, nope[:, NOPE_A:], z], axis=1)


def _pack_weights(w_in, w_uq, w_ukv):
    sizes = (A_WIDTH, A_KV_HEADS * HEAD_DIM, A_KV_HEADS * HEAD_DIM, A_WIDTH,
             Q_LORA, KV_LORA, B_ROPE, B_WIDTH)
    offs = np.cumsum((0,) + sizes)
    qa, ka, va, ga, cq, ckv, kr, gb = (w_in[:, offs[i]:offs[i + 1]] for i in range(8))
    zr = jnp.zeros((w_in.shape[0], B_NOPE), w_in.dtype)
    kr_slab = _head_slab(kr[:, :HALF_ROPE], zr, kr[:, HALF_ROPE:])
    w_in_p = jnp.concatenate(
        [qa[:, A_PAIR_COLS], ka, va, ga[:, A_PAIR_COLS], gb, cq, ckv, kr_slab], axis=1)

    per_q = B_NOPE + B_ROPE
    uq_slabs, k_slabs, v_cols = [], [], []
    zq = jnp.zeros((KV_LORA, HALF_ROPE), w_ukv.dtype)
    for hd in range(B_HEADS):
        wq = w_uq[:, hd * per_q:(hd + 1) * per_q]
        uq_slabs.append(_head_slab(wq[:, B_NOPE:B_NOPE + HALF_ROPE], wq[:, :B_NOPE],
                                   wq[:, B_NOPE + HALF_ROPE:]))
        wkv = w_ukv[:, hd * (B_NOPE + B_V):(hd + 1) * (B_NOPE + B_V)]
        k_slabs.append(_head_slab(zq, wkv[:, :B_NOPE], zq))
        v_cols.append(wkv[:, B_NOPE:])
    bf = jnp.bfloat16
    return (w_in_p.astype(bf), jnp.concatenate(uq_slabs, axis=1).astype(bf),
            jnp.concatenate(k_slabs, axis=1).astype(bf), jnp.concatenate(v_cols, axis=1).astype(bf))


def _rope_tables(seq):
    pos = jnp.arange(seq, dtype=jnp.float32)
    inv_freq = ROPE_BASE ** (-jnp.arange(0, B_ROPE, 2, dtype=jnp.float32) / B_ROPE)
    ang = pos[:, None] * inv_freq[None, :]
    cos, sin = jnp.cos(ang), jnp.sin(ang)
    one = jnp.ones((seq, B_NOPE), jnp.float32)
    zero = jnp.zeros((seq, B_NOPE), jnp.float32)
    c = _head_slab(cos, one, cos)
    s = _head_slab(-sin, zero, sin)
    return c * QB_SCALE, s * QB_SCALE, c, s


def _window_bias():
    kj = np.arange(3 * BLOCK)[:, None]
    qi = np.arange(BLOCK)[None, :]
    dist = np.abs(qi - kj + BLOCK).astype(np.float32)
    slopes = np.exp2(-8.0 * np.arange(1, A_HEADS + 1, dtype=np.float32) / A_HEADS)
    bias = -slopes[:, None, None] * dist[None] * LOG2E
    bias = np.where((dist <= WINDOW)[None], bias, NEG).astype(np.float32)
    bias = bias.reshape(A_KV_HEADS, A_REP, 3 * BLOCK, BLOCK).transpose(0, 2, 1, 3)
    bias = bias.reshape(A_KV_HEADS, 3 * BLOCK, A_REP * BLOCK)
    first, last = bias.copy(), bias.copy()
    first[:, :BLOCK] = NEG
    last[:, 2 * BLOCK:] = NEG
    return jnp.asarray(np.stack([first, bias, last]))


def kernel(x, norm_mix, w_in, attn_sink, q_a_norm, w_uq, kv_a_norm, w_ukv, w_out, final_norm):
    batch, seq, _ = x.shape
    assert norm_mix.shape[0] == 1, "single-layer block: the final norm is fused into the layer"
    assert seq % TM_PROJ == 0 and seq % TQ_LAT == 0 and (batch * seq) % TM_OUT == 0
    assert seq % (QB_WIN * BLOCK) == 0
    cq, sq, ck, sk = _rope_tables(seq)
    bias = _window_bias()
    x2 = x.reshape(batch * seq, D_MODEL)
    w_in_p, w_uq_p, w_k_p, w_v = _pack_weights(w_in[0], w_uq[0], w_ukv[0])
    qa, kva, gate, qb, kb, vb = _input_proj(
        x2, norm_mix, w_in_p, q_a_norm, w_uq_p, kv_a_norm, w_k_p, w_v, cq, sq, ck, sk, seq)
    sink_tab = jnp.repeat(attn_sink[0].astype(jnp.float32) * LOG2E, BLOCK).reshape(A_KV_HEADS, A_REP * BLOCK)
    ya = _window_attn(qa, kva, gate, bias, sink_tab, batch, seq)
    yb = _latent_attn(qb, kb, vb, gate, batch, seq)
    w_o = w_out[0].astype(jnp.bfloat16)
    out = _output_proj(ya, yb, x2, w_o[:A_WIDTH][A_PAIR_COLS], w_o[A_WIDTH:], final_norm[None])
    return out.reshape(batch, seq, D_MODEL)
```

```python
import functools

import jax
import jax.numpy as jnp
import numpy as np
from jax.experimental import pallas as pl
from jax.experimental.pallas import tpu as pltpu

D_MODEL = 1024
HEAD_DIM = 64
BLOCK = 128
A_HEADS = 8
A_KV_HEADS = 2
A_REP = A_HEADS // A_KV_HEADS
A_WIDTH = A_HEADS * HEAD_DIM
WINDOW = 128
B_HEADS = 8
B_NOPE = 64
B_ROPE = 32
B_V = 64
B_WIDTH = B_HEADS * B_V
Q_LORA = 384
KV_LORA = 256
ROPE_BASE = 10000.0
EPS = 1e-6

LANES = 128
HALF_ROPE = B_ROPE // 2
NEG = -1e30
LOG2E = float(np.log2(np.e))
QA_SCALE = HEAD_DIM ** -0.5 * LOG2E
QB_SCALE = (B_NOPE + B_ROPE) ** -0.5 * LOG2E
VMEM_LIMIT = 56 * 1024 * 1024

NOPE_A = 48

A_PAIR_COLS = np.concatenate(
    [np.arange(HEAD_DIM) + (g * A_REP + p) * HEAD_DIM for p in range(A_REP) for g in range(A_KV_HEADS)])

C_QA = 0
C_KV = C_QA + A_WIDTH
C_GATE = C_KV + 2 * A_KV_HEADS * HEAD_DIM
C_CQ = C_GATE + A_WIDTH + B_WIDTH
C_CKV = C_CQ + Q_LORA
C_KR = C_CKV + KV_LORA
C_END = C_KR + LANES

TM_PROJ = 512
QB_WIN = 4
TQ_LAT = 512
CK_LAT = 256
TM_OUT = 512


def _rms(x, gain):
    return x * jax.lax.rsqrt(jnp.mean(x * x, axis=-1, keepdims=True) + EPS) * gain


def _rope(slab, cos, sin):
    return slab * cos + pltpu.roll(slab, 64, 1) * sin


def _proj_kernel(x_ref, g_ref, win_ref, qg_ref, wuq_ref, kvg_ref, wk_ref, wvt_ref,
                 cq_ref, sq_ref, ck_ref, sk_ref,
                 qa_ref, kva_ref, gate_ref, qb_ref, kb_ref, vbt_ref):
    h = _rms(x_ref[...], g_ref[...]).astype(jnp.bfloat16)

    def seg(lo, hi):
        return jnp.dot(h, win_ref[:, lo:hi], preferred_element_type=jnp.float32)

    qa_ref[...] = (seg(C_QA, C_KV) * QA_SCALE).astype(jnp.bfloat16)
    kva_ref[...] = seg(C_KV, C_GATE).astype(jnp.bfloat16)
    g = seg(C_GATE, C_CQ)
    gate_ref[...] = (g / (1.0 + jnp.exp(-g))).astype(jnp.bfloat16)

    cqn = _rms(seg(C_CQ, C_CKV), qg_ref[...]).astype(jnp.bfloat16)
    q = jnp.dot(cqn, wuq_ref[...], preferred_element_type=jnp.float32)
    cq, sq = cq_ref[...], sq_ref[...]
    for hd in range(B_HEADS):
        sl = slice(hd * LANES, (hd + 1) * LANES)
        qb_ref[:, sl] = _rope(q[:, sl], cq, sq).astype(jnp.bfloat16)

    ckvn = _rms(seg(C_CKV, C_KR), kvg_ref[...]).astype(jnp.bfloat16)
    kr = _rope(seg(C_KR, C_END), ck_ref[...], sk_ref[...])
    kn = jnp.dot(ckvn, wk_ref[...], preferred_element_type=jnp.float32)
    for hd in range(B_HEADS):
        sl = slice(hd * LANES, (hd + 1) * LANES)
        kb_ref[:, sl] = (kn[:, sl] + kr).astype(jnp.bfloat16)
    vbt_ref[...] = jax.lax.dot_general(wvt_ref[...], ckvn, (((1,), (1,)), ((), ())),
                                       preferred_element_type=jnp.float32).astype(jnp.bfloat16)


def _input_proj(x2, norm_g, w_in_p, q_g, w_uq_p, kv_g, w_k_p, w_vt, cq, sq, ck, sk, seq):
    t = x2.shape[0]
    tm = TM_PROJ
    n_pos = seq // tm
    full = lambda a: pl.BlockSpec(a.shape, lambda i: (0,) * a.ndim)
    rows = lambda w: pl.BlockSpec((tm, w), lambda i: (i, 0))
    tab = pl.BlockSpec((tm, LANES), lambda i: (i % n_pos, 0))
    bf = jnp.bfloat16
    widths = (A_WIDTH, 2 * A_KV_HEADS * HEAD_DIM, A_WIDTH + B_WIDTH,
              B_HEADS * LANES, B_HEADS * LANES)
    return pl.pallas_call(
        _proj_kernel,
        grid=(t // tm,),
        in_specs=[rows(D_MODEL), full(norm_g), full(w_in_p), full(q_g), full(w_uq_p),
                  full(kv_g), full(w_k_p), full(w_vt), tab, tab, tab, tab],
        out_specs=[rows(w) for w in widths]
        + [pl.BlockSpec((None, B_WIDTH, tm), lambda i: (i // n_pos, 0, i % n_pos))],
        out_shape=[jax.ShapeDtypeStruct((t, w), bf) for w in widths]
        + [jax.ShapeDtypeStruct((t // seq, B_WIDTH, seq), bf)],
        compiler_params=pltpu.CompilerParams(
            dimension_semantics=("arbitrary",), vmem_limit_bytes=VMEM_LIMIT),
        name="input_proj",
    )(x2, norm_g, w_in_p, q_g, w_uq_p, kv_g, w_k_p, w_vt, cq, sq, ck, sk)


def _window_block(q_ref, kva_ref, vt_ref, gate_ref, bias_ref, sink_ref, o_ref,
                  rows, n, bias_idx, n_blocks):
    band = (jnp.maximum(n - 1, 0), n, jnp.minimum(n + 1, n_blocks - 1))
    kk = jnp.concatenate(
        [kva_ref[pl.ds(pl.multiple_of(b * BLOCK, BLOCK), BLOCK), :LANES] for b in band],
        axis=0)
    vts = [vt_ref[b] for b in band]
    first = jax.lax.broadcasted_iota(jnp.int32, (BLOCK, LANES), 1) < HEAD_DIM
    zero = jnp.zeros((), jnp.bfloat16)
    qp = [q_ref[rows, p * LANES:(p + 1) * LANES] for p in range(A_REP)]

    outs = []
    for g in range(A_KV_HEADS):
        keep = first if g == 0 else jnp.logical_not(first)
        q4 = jnp.concatenate([jnp.where(keep, x, zero) for x in qp], axis=0)
        st = jax.lax.dot_general(kk, q4, (((1,), (1,)), ((), ())),
                                 preferred_element_type=jnp.float32)
        st = st + bias_ref[bias_idx, g]
        sink = sink_ref[g:g + 1, :]
        m = jnp.maximum(jnp.max(st, axis=0, keepdims=True), sink)
        p = jnp.exp2(st - m)
        inv = 1.0 / (jnp.sum(p, axis=0, keepdims=True) + jnp.exp2(sink - m))
        vt = jnp.concatenate([x[g * HEAD_DIM:(g + 1) * HEAD_DIM, :] for x in vts], axis=1)
        ot = jnp.dot(vt, p.astype(jnp.bfloat16), preferred_element_type=jnp.float32)
        outs.append(ot * inv)

    for pr in range(A_REP):
        sl = slice(pr * LANES, (pr + 1) * LANES)
        o = jnp.concatenate([outs[0][:, sl], outs[1][:, sl]], axis=0).T
        o_ref[rows, sl] = (o * gate_ref[rows, sl].astype(jnp.float32)).astype(jnp.bfloat16)


def _window_kernel(q_ref, kva_ref, gate_ref, bias_ref, sink_ref, o_ref, vt_ref, *, n_blocks):
    step = pl.program_id(1)
    n_steps = n_blocks // QB_WIN

    @pl.when(step == 0)
    def _():
        for b in range(n_blocks):
            v = kva_ref[b * BLOCK:(b + 1) * BLOCK, LANES:].astype(jnp.float32)
            vt_ref[b] = v.T.astype(jnp.bfloat16)

    for blk in range(QB_WIN):
        if blk == 0:
            bias_idx = jnp.where(step == 0, 0, 1)
        elif blk == QB_WIN - 1:
            bias_idx = jnp.where(step == n_steps - 1, 2, 1)
        else:
            bias_idx = 1
        _window_block(q_ref, kva_ref, vt_ref, gate_ref, bias_ref, sink_ref, o_ref,
                      pl.ds(blk * BLOCK, BLOCK), step * QB_WIN + blk, bias_idx, n_blocks)


def _window_attn(qa, kva, gate, bias, sink_tab, batch, seq):
    nb = seq // BLOCK
    ns = nb // QB_WIN
    tq = QB_WIN * BLOCK
    return pl.pallas_call(
        functools.partial(_window_kernel, n_blocks=nb),
        grid=(batch, ns),
        in_specs=[pl.BlockSpec((tq, A_WIDTH), lambda b, n: (b * ns + n, 0)),
                  pl.BlockSpec((seq, 2 * A_KV_HEADS * HEAD_DIM), lambda b, n: (b, 0)),
                  pl.BlockSpec((tq, A_WIDTH), lambda b, n: (b * ns + n, 0)),
                  pl.BlockSpec(bias.shape, lambda b, n: (0, 0, 0, 0)),
                  pl.BlockSpec(sink_tab.shape, lambda b, n: (0, 0))],
        out_specs=pl.BlockSpec((tq, A_WIDTH), lambda b, n: (b * ns + n, 0)),
        out_shape=jax.ShapeDtypeStruct((batch * seq, A_WIDTH), jnp.bfloat16),
        scratch_shapes=[pltpu.VMEM((nb, LANES, BLOCK), jnp.bfloat16)],
        compiler_params=pltpu.CompilerParams(
            dimension_semantics=("arbitrary", "arbitrary"), vmem_limit_bytes=VMEM_LIMIT),
        name="window_attn",
    )(qa, kva, gate, bias, sink_tab)


def _latent_kernel(q_ref, k_ref, vt_ref, gate_ref, o_ref):
    n_chunks = k_ref.shape[0] // CK_LAT
    items = [(hd, c) for hd in range(B_HEADS) for c in range(n_chunks)]

    def scores(hd, c):
        sl = slice(hd * LANES, (hd + 1) * LANES)
        return jax.lax.dot_general(k_ref[c * CK_LAT:(c + 1) * CK_LAT, sl], q_ref[:, sl],
                                   (((1,), (1,)), ((), ())),
                                   preferred_element_type=jnp.float32)

    outs = []
    s_next = scores(*items[0])
    for idx, (hd, c) in enumerate(items):
        s = s_next
        if idx + 1 < len(items):
            s_next = scores(*items[idx + 1])
        vt = vt_ref[hd * B_V:(hd + 1) * B_V, c * CK_LAT:(c + 1) * CK_LAT]
        cmax = jnp.max(s, axis=0, keepdims=True)
        if c == 0:
            m = cmax
            p = jnp.exp2(s - m)
            l = jnp.sum(p, axis=0, keepdims=True)
            acc = jnp.dot(vt, p.astype(jnp.bfloat16), preferred_element_type=jnp.float32)
        else:
            m_new = jnp.maximum(m, cmax)
            alpha = jnp.exp2(m - m_new)
            p = jnp.exp2(s - m_new)
            l = l * alpha + jnp.sum(p, axis=0, keepdims=True)
            acc = acc * alpha + jnp.dot(vt, p.astype(jnp.bfloat16),
                                        preferred_element_type=jnp.float32)
            m = m_new
        if c == n_chunks - 1:
            outs.append(acc * (1.0 / l))
            if hd % 2 == 1:
                o = jnp.concatenate(outs, axis=0).T
                outs = []
                sl = slice((hd // 2) * LANES, (hd // 2 + 1) * LANES)
                o_ref[:, sl] = (o * gate_ref[:, sl].astype(jnp.float32)).astype(jnp.bfloat16)


def _latent_attn(qb, kb, vbt, gate, batch, seq):
    tq = TQ_LAT
    nq = seq // tq
    return pl.pallas_call(
        _latent_kernel,
        grid=(batch, nq),
        in_specs=[pl.BlockSpec((tq, B_HEADS * LANES), lambda b, i: (b * nq + i, 0)),
                  pl.BlockSpec((seq, B_HEADS * LANES), lambda b, i: (b, 0)),
                  pl.BlockSpec((None, B_WIDTH, seq), lambda b, i: (b, 0, 0)),
                  pl.BlockSpec((tq, B_WIDTH), lambda b, i: (b * nq + i, 1))],
        out_specs=pl.BlockSpec((tq, B_WIDTH), lambda b, i: (b * nq + i, 0)),
        out_shape=jax.ShapeDtypeStruct((batch * seq, B_WIDTH), jnp.bfloat16),
        compiler_params=pltpu.CompilerParams(
            dimension_semantics=("arbitrary", "arbitrary"), vmem_limit_bytes=VMEM_LIMIT),
        name="latent_attn",
    )(qb, kb, vbt, gate)


def _out_kernel(ya_ref, yb_ref, x_ref, wa_ref, wb_ref, g_ref, o_ref):
    d = jnp.dot(ya_ref[...], wa_ref[...], preferred_element_type=jnp.float32)
    d = d + jnp.dot(yb_ref[...], wb_ref[...], preferred_element_type=jnp.float32)
    o_ref[...] = _rms(x_ref[...] + d, g_ref[...])


def _output_proj(ya, yb, x2, w_a, w_b, final_g):
    t = x2.shape[0]
    tm = TM_OUT
    full = lambda a: pl.BlockSpec(a.shape, lambda i: (0,) * a.ndim)
    rows = lambda w: pl.BlockSpec((tm, w), lambda i: (i, 0))
    return pl.pallas_call(
        _out_kernel,
        grid=(t // tm,),
        in_specs=[rows(A_WIDTH), rows(B_WIDTH), rows(D_MODEL), full(w_a), full(w_b), full(final_g)],
        out_specs=rows(D_MODEL),
        out_shape=jax.ShapeDtypeStruct((t, D_MODEL), jnp.float32),
        compiler_params=pltpu.CompilerParams(
            dimension_semantics=("arbitrary",), vmem_limit_bytes=VMEM_LIMIT),
        name="output_proj",
    )(ya, yb, x2, w_a, w_b, final_g)


def _head_slab(x1, nope, x2):
    z = jnp.zeros((nope.shape[0], LANES - B_NOPE - B_ROPE), nope.dtype)
    return jnp.concatenate([x1, nope[:, :NOPE_A], x2, nope[:, NOPE_A:], z], axis=1)


def _pack_weights(w_in, w_uq, w_ukv):
    sizes = (A_WIDTH, A_KV_HEADS * HEAD_DIM, A_KV_HEADS * HEAD_DIM, A_WIDTH,
             Q_LORA, KV_LORA, B_ROPE, B_WIDTH)
    offs = np.cumsum((0,) + sizes)
    qa, ka, va, ga, cq, ckv, kr, gb = (w_in[:, offs[i]:offs[i + 1]] for i in range(8))
    zr = jnp.zeros((w_in.shape[0], B_NOPE), w_in.dtype)
    kr_slab = _head_slab(kr[:, :HALF_ROPE], zr, kr[:, HALF_ROPE:])
    w_in_p = jnp.concatenate(
        [qa[:, A_PAIR_COLS], ka, va, ga[:, A_PAIR_COLS], gb, cq, ckv, kr_slab], axis=1)

    per_q = B_NOPE + B_ROPE
    uq_slabs, k_slabs, v_cols = [], [], []
    zq = jnp.zeros((KV_LORA, HALF_ROPE), w_ukv.dtype)
    for hd in range(B_HEADS):
        wq = w_uq[:, hd * per_q:(hd + 1) * per_q]
        uq_slabs.append(_head_slab(wq[:, B_NOPE:B_NOPE + HALF_ROPE], wq[:, :B_NOPE],
                                   wq[:, B_NOPE + HALF_ROPE:]))
        wkv = w_ukv[:, hd * (B_NOPE + B_V):(hd + 1) * (B_NOPE + B_V)]
        k_slabs.append(_head_slab(zq, wkv[:, :B_NOPE], zq))
        v_cols.append(wkv[:, B_NOPE:])
    bf = jnp.bfloat16
    return (w_in_p.astype(bf), jnp.concatenate(uq_slabs, axis=1).astype(bf),
            jnp.concatenate(k_slabs, axis=1).astype(bf), jnp.concatenate(v_cols, axis=1).T.astype(bf))


def _rope_tables(seq):
    pos = jnp.arange(seq, dtype=jnp.float32)
    inv_freq = ROPE_BASE ** (-jnp.arange(0, B_ROPE, 2, dtype=jnp.float32) / B_ROPE)
    ang = pos[:, None] * inv_freq[None, :]
    cos, sin = jnp.cos(ang), jnp.sin(ang)
    one = jnp.ones((seq, B_NOPE), jnp.float32)
    zero = jnp.zeros((seq, B_NOPE), jnp.float32)
    c = _head_slab(cos, one, cos)
    s = _head_slab(-sin, zero, sin)
    return c * QB_SCALE, s * QB_SCALE, c, s


def _window_bias():
    kj = np.arange(3 * BLOCK)[:, None]
    qi = np.arange(BLOCK)[None, :]
    dist = np.abs(qi - kj + BLOCK).astype(np.float32)
    slopes = np.exp2(-8.0 * np.arange(1, A_HEADS + 1, dtype=np.float32) / A_HEADS)
    bias = -slopes[:, None, None] * dist[None] * LOG2E
    bias = np.where((dist <= WINDOW)[None], bias, NEG).astype(np.float32)
    bias = bias.reshape(A_KV_HEADS, A_REP, 3 * BLOCK, BLOCK).transpose(0, 2, 1, 3)
    bias = bias.reshape(A_KV_HEADS, 3 * BLOCK, A_REP * BLOCK)
    first, last = bias.copy(), bias.copy()
    first[:, :BLOCK] = NEG
    last[:, 2 * BLOCK:] = NEG
    return jnp.asarray(np.stack([first, bias, last]))


def kernel(x, norm_mix, w_in, attn_sink, q_a_norm, w_uq, kv_a_norm, w_ukv, w_out, final_norm):
    batch, seq, _ = x.shape
    assert norm_mix.shape[0] == 1, "single-layer block: the final norm is fused into the layer"
    assert seq % TM_PROJ == 0 and seq % TQ_LAT == 0 and (batch * seq) % TM_OUT == 0
    assert seq % (QB_WIN * BLOCK) == 0
    cq, sq, ck, sk = _rope_tables(seq)
    bias = _window_bias()
    x2 = x.reshape(batch * seq, D_MODEL)
    w_in_p, w_uq_p, w_k_p, w_vt = _pack_weights(w_in[0], w_uq[0], w_ukv[0])
    qa, kva, gate, qb, kb, vbt = _input_proj(
        x2, norm_mix, w_in_p, q_a_norm, w_uq_p, kv_a_norm, w_k_p, w_vt, cq, sq, ck, sk, seq)
    sink_tab = jnp.repeat(attn_sink[0].astype(jnp.float32) * LOG2E, BLOCK).reshape(A_KV_HEADS, A_REP * BLOCK)
    ya = _window_attn(qa, kva, gate, bias, sink_tab, batch, seq)
    yb = _latent_attn(qb, kb, vbt, gate, batch, seq)
    w_o = w_out[0].astype(jnp.bfloat16)
    out = _output_proj(ya, yb, x2, w_o[:A_WIDTH][A_PAIR_COLS], w_o[A_WIDTH:], final_norm[None])
    return out.reshape(batch, seq, D_MODEL)
```

```python
import functools

import jax
import jax.numpy as jnp
import numpy as np
from jax.experimental import pallas as pl
from jax.experimental.pallas import tpu as pltpu

D_MODEL = 1024
HEAD_DIM = 64
BLOCK = 128
A_HEADS = 8
A_KV_HEADS = 2
A_REP = A_HEADS // A_KV_HEADS
A_WIDTH = A_HEADS * HEAD_DIM
WINDOW = 128
B_HEADS = 8
B_NOPE = 64
B_ROPE = 32
B_V = 64
B_WIDTH = B_HEADS * B_V
Q_LORA = 384
KV_LORA = 256
ROPE_BASE = 10000.0
EPS = 1e-6

LANES = 128
HALF_ROPE = B_ROPE // 2
NEG = -1e30
LOG2E = float(np.log2(np.e))
QA_SCALE = HEAD_DIM ** -0.5 * LOG2E
QB_SCALE = (B_NOPE + B_ROPE) ** -0.5 * LOG2E
VMEM_LIMIT = 56 * 1024 * 1024

NOPE_A = 48

A_PAIR_COLS = np.concatenate(
    [np.arange(HEAD_DIM) + (g * A_REP + p) * HEAD_DIM for p in range(A_REP) for g in range(A_KV_HEADS)])

C_QA = 0
C_KV = C_QA + A_WIDTH
C_GATE = C_KV + 2 * A_KV_HEADS * HEAD_DIM
C_CQ = C_GATE + A_WIDTH + B_WIDTH
C_CKV = C_CQ + Q_LORA
C_KR = C_CKV + KV_LORA
C_END = C_KR + LANES

TM_PROJ = 512
QB_WIN = 4
TQ_LAT = 512
CK_LAT = 256


def _rms(x, gain):
    return x * jax.lax.rsqrt(jnp.mean(x * x, axis=-1, keepdims=True) + EPS) * gain


def _rope(slab, cos, sin):
    return slab * cos + pltpu.roll(slab, 64, 1) * sin


def _proj_kernel(x_ref, g_ref, win_ref, qg_ref, wuq_ref, kvg_ref, wk_ref, wvt_ref,
                 cq_ref, sq_ref, ck_ref, sk_ref,
                 qa_ref, kva_ref, gate_ref, qb_ref, kb_ref, vbt_ref):
    h = _rms(x_ref[...], g_ref[...]).astype(jnp.bfloat16)

    def seg(lo, hi):
        return jnp.dot(h, win_ref[:, lo:hi], preferred_element_type=jnp.float32)

    qa_ref[...] = (seg(C_QA, C_KV) * QA_SCALE).astype(jnp.bfloat16)
    kva_ref[...] = seg(C_KV, C_GATE).astype(jnp.bfloat16)
    g = seg(C_GATE, C_CQ)
    gate_ref[...] = (g / (1.0 + jnp.exp(-g))).astype(jnp.bfloat16)

    cqn = _rms(seg(C_CQ, C_CKV), qg_ref[...]).astype(jnp.bfloat16)
    q = jnp.dot(cqn, wuq_ref[...], preferred_element_type=jnp.float32)
    cq, sq = cq_ref[...], sq_ref[...]
    for hd in range(B_HEADS):
        sl = slice(hd * LANES, (hd + 1) * LANES)
        qb_ref[:, sl] = _rope(q[:, sl], cq, sq).astype(jnp.bfloat16)

    ckvn = _rms(seg(C_CKV, C_KR), kvg_ref[...]).astype(jnp.bfloat16)
    kr = _rope(seg(C_KR, C_END), ck_ref[...], sk_ref[...])
    kn = jnp.dot(ckvn, wk_ref[...], preferred_element_type=jnp.float32)
    for hd in range(B_HEADS):
        sl = slice(hd * LANES, (hd + 1) * LANES)
        kb_ref[:, sl] = (kn[:, sl] + kr).astype(jnp.bfloat16)
    vbt_ref[...] = jax.lax.dot_general(wvt_ref[...], ckvn, (((1,), (1,)), ((), ())),
                                       preferred_element_type=jnp.float32).astype(jnp.bfloat16)


def _input_proj(x2, norm_g, w_in_p, q_g, w_uq_p, kv_g, w_k_p, w_vt, cq, sq, ck, sk, seq):
    t = x2.shape[0]
    tm = TM_PROJ
    n_pos = seq // tm
    full = lambda a: pl.BlockSpec(a.shape, lambda i: (0,) * a.ndim)
    rows = lambda w: pl.BlockSpec((tm, w), lambda i: (i, 0))
    tab = pl.BlockSpec((tm, LANES), lambda i: (i % n_pos, 0))
    bf = jnp.bfloat16
    widths = (A_WIDTH, 2 * A_KV_HEADS * HEAD_DIM, A_WIDTH + B_WIDTH,
              B_HEADS * LANES, B_HEADS * LANES)
    return pl.pallas_call(
        _proj_kernel,
        grid=(t // tm,),
        in_specs=[rows(D_MODEL), full(norm_g), full(w_in_p), full(q_g), full(w_uq_p),
                  full(kv_g), full(w_k_p), full(w_vt), tab, tab, tab, tab],
        out_specs=[rows(w) for w in widths]
        + [pl.BlockSpec((None, B_WIDTH, tm), lambda i: (i // n_pos, 0, i % n_pos))],
        out_shape=[jax.ShapeDtypeStruct((t, w), bf) for w in widths]
        + [jax.ShapeDtypeStruct((t // seq, B_WIDTH, seq), bf)],
        compiler_params=pltpu.CompilerParams(
            dimension_semantics=("arbitrary",), vmem_limit_bytes=VMEM_LIMIT),
        name="input_proj",
    )(x2, norm_g, w_in_p, q_g, w_uq_p, kv_g, w_k_p, w_vt, cq, sq, ck, sk)


def _window_block(q_ref, kva_ref, vt_ref, gate_ref, bias_ref, sink_ref, o_ref,
                  rows, n, bias_idx, n_blocks, between=()):
    band = (jnp.maximum(n - 1, 0), n, jnp.minimum(n + 1, n_blocks - 1))
    kk = jnp.concatenate(
        [kva_ref[pl.ds(pl.multiple_of(b * BLOCK, BLOCK), BLOCK), :LANES] for b in band],
        axis=0)
    vts = [vt_ref[b] for b in band]
    first = jax.lax.broadcasted_iota(jnp.int32, (BLOCK, LANES), 1) < HEAD_DIM
    zero = jnp.zeros((), jnp.bfloat16)
    qp = [q_ref[rows, p * LANES:(p + 1) * LANES] for p in range(A_REP)]

    outs = []
    for g in range(A_KV_HEADS):
        keep = first if g == 0 else jnp.logical_not(first)
        q4 = jnp.concatenate([jnp.where(keep, x, zero) for x in qp], axis=0)
        st = jax.lax.dot_general(kk, q4, (((1,), (1,)), ((), ())),
                                 preferred_element_type=jnp.float32)
        st = st + bias_ref[bias_idx, g]
        sink = sink_ref[g:g + 1, :]
        m = jnp.maximum(jnp.max(st, axis=0, keepdims=True), sink)
        p = jnp.exp2(st - m)
        inv = 1.0 / (jnp.sum(p, axis=0, keepdims=True) + jnp.exp2(sink - m))
        vt = jnp.concatenate([x[g * HEAD_DIM:(g + 1) * HEAD_DIM, :] for x in vts], axis=1)
        ot = jnp.dot(vt, p.astype(jnp.bfloat16), preferred_element_type=jnp.float32)
        outs.append(ot * inv)
        if g < len(between):
            between[g]()

    for pr in range(A_REP):
        sl = slice(pr * LANES, (pr + 1) * LANES)
        o = jnp.concatenate([outs[0][:, sl], outs[1][:, sl]], axis=0).T
        o_ref[rows, sl] = (o * gate_ref[rows, sl].astype(jnp.float32)).astype(jnp.bfloat16)


def _window_out_kernel(q_ref, kva_ref, gate_ref, bias_ref, sink_ref, yb_ref, x_ref, wa_ref, wb_ref,
                       g_ref, o_ref, vt_ref, ya_ref, *, n_blocks):
    step = pl.program_id(1)
    n_steps = n_blocks // QB_WIN

    @pl.when(step == 0)
    def _():
        for b in range(n_blocks):
            v = kva_ref[b * BLOCK:(b + 1) * BLOCK, LANES:].astype(jnp.float32)
            vt_ref[b] = v.T.astype(jnp.bfloat16)

    def out_proj(rows, half):
        def emit():
            cols = slice(half * (D_MODEL // 2), (half + 1) * (D_MODEL // 2))
            d = jnp.dot(ya_ref[rows, :], wa_ref[:, cols], preferred_element_type=jnp.float32)
            d = d + jnp.dot(yb_ref[rows, :], wb_ref[:, cols], preferred_element_type=jnp.float32)
            o_ref[rows, cols] = x_ref[rows, cols] + d
        return emit

    def finish(rows):
        def emit():
            out_proj(rows, 1)()
            o_ref[rows, :] = _rms(o_ref[rows, :], g_ref[...])
        return emit

    pending = ()
    for blk in range(QB_WIN):
        if blk == 0:
            bias_idx = jnp.where(step == 0, 0, 1)
        elif blk == QB_WIN - 1:
            bias_idx = jnp.where(step == n_steps - 1, 2, 1)
        else:
            bias_idx = 1
        rows = pl.ds(blk * BLOCK, BLOCK)
        _window_block(q_ref, kva_ref, vt_ref, gate_ref, bias_ref, sink_ref, ya_ref,
                      rows, step * QB_WIN + blk, bias_idx, n_blocks, between=pending)
        pending = (out_proj(rows, 0), finish(rows))
    for emit in pending:
        emit()


def _window_attn_out(qa, kva, gate, bias, sink_tab, yb, x2, w_a, w_b, final_g, batch, seq):
    nb = seq // BLOCK
    ns = nb // QB_WIN
    tq = QB_WIN * BLOCK
    rows = lambda w: pl.BlockSpec((tq, w), lambda b, n: (b * ns + n, 0))
    full = lambda a: pl.BlockSpec(a.shape, lambda b, n: (0,) * a.ndim)
    return pl.pallas_call(
        functools.partial(_window_out_kernel, n_blocks=nb),
        grid=(batch, ns),
        in_specs=[rows(A_WIDTH),
                  pl.BlockSpec((seq, 2 * A_KV_HEADS * HEAD_DIM), lambda b, n: (b, 0)),
                  rows(A_WIDTH), full(bias), full(sink_tab), rows(B_WIDTH), rows(D_MODEL),
                  full(w_a), full(w_b), full(final_g)],
        out_specs=rows(D_MODEL),
        out_shape=jax.ShapeDtypeStruct((batch * seq, D_MODEL), jnp.float32),
        scratch_shapes=[pltpu.VMEM((nb, LANES, BLOCK), jnp.bfloat16),
                        pltpu.VMEM((tq, A_WIDTH), jnp.bfloat16)],
        compiler_params=pltpu.CompilerParams(
            dimension_semantics=("arbitrary", "arbitrary"), vmem_limit_bytes=VMEM_LIMIT),
        name="window_attn_out",
    )(qa, kva, gate, bias, sink_tab, yb, x2, w_a, w_b, final_g)


def _latent_kernel(q_ref, k_ref, vt_ref, gate_ref, o_ref):
    n_chunks = k_ref.shape[0] // CK_LAT
    items = [(hd, c) for hd in range(B_HEADS) for c in range(n_chunks)]

    def scores(hd, c):
        sl = slice(hd * LANES, (hd + 1) * LANES)
        return jax.lax.dot_general(k_ref[c * CK_LAT:(c + 1) * CK_LAT, sl], q_ref[:, sl],
                                   (((1,), (1,)), ((), ())),
                                   preferred_element_type=jnp.float32)

    outs = []
    s_next = scores(*items[0])
    for idx, (hd, c) in enumerate(items):
        s = s_next
        if idx + 1 < len(items):
            s_next = scores(*items[idx + 1])
        vt = vt_ref[hd * B_V:(hd + 1) * B_V, c * CK_LAT:(c + 1) * CK_LAT]
        cmax = jnp.max(s, axis=0, keepdims=True)
        if c == 0:
            m = cmax
            p = jnp.exp2(s - m)
            l = jnp.sum(p, axis=0, keepdims=True)
            acc = jnp.dot(vt, p.astype(jnp.bfloat16), preferred_element_type=jnp.float32)
        else:
            m_new = jnp.maximum(m, cmax)
            alpha = jnp.exp2(m - m_new)
            p = jnp.exp2(s - m_new)
            l = l * alpha + jnp.sum(p, axis=0, keepdims=True)
            acc = acc * alpha + jnp.dot(vt, p.astype(jnp.bfloat16),
                                        preferred_element_type=jnp.float32)
            m = m_new
        if c == n_chunks - 1:
            outs.append(acc * (1.0 / l))
            if hd % 2 == 1:
                o = jnp.concatenate(outs, axis=0).T
                outs = []
                sl = slice((hd // 2) * LANES, (hd // 2 + 1) * LANES)
                o_ref[:, sl] = (o * gate_ref[:, sl].astype(jnp.float32)).astype(jnp.bfloat16)


def _latent_attn(qb, kb, vbt, gate, batch, seq):
    tq = TQ_LAT
    nq = seq // tq
    return pl.pallas_call(
        _latent_kernel,
        grid=(batch, nq),
        in_specs=[pl.BlockSpec((tq, B_HEADS * LANES), lambda b, i: (b * nq + i, 0)),
                  pl.BlockSpec((seq, B_HEADS * LANES), lambda b, i: (b, 0)),
                  pl.BlockSpec((None, B_WIDTH, seq), lambda b, i: (b, 0, 0)),
                  pl.BlockSpec((tq, B_WIDTH), lambda b, i: (b * nq + i, 1))],
        out_specs=pl.BlockSpec((tq, B_WIDTH), lambda b, i: (b * nq + i, 0)),
        out_shape=jax.ShapeDtypeStruct((batch * seq, B_WIDTH), jnp.bfloat16),
        compiler_params=pltpu.CompilerParams(
            dimension_semantics=("arbitrary", "arbitrary"), vmem_limit_bytes=VMEM_LIMIT),
        name="latent_attn",
    )(qb, kb, vbt, gate)


def _head_slab(x1, nope, x2):
    z = jnp.zeros((nope.shape[0], LANES - B_NOPE - B_ROPE), nope.dtype)
    return jnp.concatenate([x1, nope[:, :NOPE_A], x2, nope[:, NOPE_A:], z], axis=1)


def _pack_weights(w_in, w_uq, w_ukv):
    sizes = (A_WIDTH, A_KV_HEADS * HEAD_DIM, A_KV_HEADS * HEAD_DIM, A_WIDTH,
             Q_LORA, KV_LORA, B_ROPE, B_WIDTH)
    offs = np.cumsum((0,) + sizes)
    qa, ka, va, ga, cq, ckv, kr, gb = (w_in[:, offs[i]:offs[i + 1]] for i in range(8))
    zr = jnp.zeros((w_in.shape[0], B_NOPE), w_in.dtype)
    kr_slab = _head_slab(kr[:, :HALF_ROPE], zr, kr[:, HALF_ROPE:])
    w_in_p = jnp.concatenate(
        [qa[:, A_PAIR_COLS], ka, va, ga[:, A_PAIR_COLS], gb, cq, ckv, kr_slab], axis=1)

    per_q = B_NOPE + B_ROPE
    uq_slabs, k_slabs, v_cols = [], [], []
    zq = jnp.zeros((KV_LORA, HALF_ROPE), w_ukv.dtype)
    for hd in range(B_HEADS):
        wq = w_uq[:, hd * per_q:(hd + 1) * per_q]
        uq_slabs.append(_head_slab(wq[:, B_NOPE:B_NOPE + HALF_ROPE], wq[:, :B_NOPE],
                                   wq[:, B_NOPE + HALF_ROPE:]))
        wkv = w_ukv[:, hd * (B_NOPE + B_V):(hd + 1) * (B_NOPE + B_V)]
        k_slabs.append(_head_slab(zq, wkv[:, :B_NOPE], zq))
        v_cols.append(wkv[:, B_NOPE:])
    bf = jnp.bfloat16
    return (w_in_p.astype(bf), jnp.concatenate(uq_slabs, axis=1).astype(bf),
            jnp.concatenate(k_slabs, axis=1).astype(bf), jnp.concatenate(v_cols, axis=1).T.astype(bf))


def _rope_tables(seq):
    pos = jnp.arange(seq, dtype=jnp.float32)
    inv_freq = ROPE_BASE ** (-jnp.arange(0, B_ROPE, 2, dtype=jnp.float32) / B_ROPE)
    ang = pos[:, None] * inv_freq[None, :]
    cos, sin = jnp.cos(ang), jnp.sin(ang)
    one = jnp.ones((seq, B_NOPE), jnp.float32)
    zero = jnp.zeros((seq, B_NOPE), jnp.float32)
    c = _head_slab(cos, one, cos)
    s = _head_slab(-sin, zero, sin)
    return c * QB_SCALE, s * QB_SCALE, c, s


def _window_bias():
    kj = np.arange(3 * BLOCK)[:, None]
    qi = np.arange(BLOCK)[None, :]
    dist = np.abs(qi - kj + BLOCK).astype(np.float32)
    slopes = np.exp2(-8.0 * np.arange(1, A_HEADS + 1, dtype=np.float32) / A_HEADS)
    bias = -slopes[:, None, None] * dist[None] * LOG2E
    bias = np.where((dist <= WINDOW)[None], bias, NEG).astype(np.float32)
    bias = bias.reshape(A_KV_HEADS, A_REP, 3 * BLOCK, BLOCK).transpose(0, 2, 1, 3)
    bias = bias.reshape(A_KV_HEADS, 3 * BLOCK, A_REP * BLOCK)
    first, last = bias.copy(), bias.copy()
    first[:, :BLOCK] = NEG
    last[:, 2 * BLOCK:] = NEG
    return jnp.asarray(np.stack([first, bias, last]))


def kernel(x, norm_mix, w_in, attn_sink, q_a_norm, w_uq, kv_a_norm, w_ukv, w_out, final_norm):
    batch, seq, _ = x.shape
    assert norm_mix.shape[0] == 1, "single-layer block: the final norm is fused into the layer"
    assert seq % TM_PROJ == 0 and seq % TQ_LAT == 0 and seq % CK_LAT == 0
    assert seq % (QB_WIN * BLOCK) == 0
    cq, sq, ck, sk = _rope_tables(seq)
    bias = _window_bias()
    x2 = x.reshape(batch * seq, D_MODEL)
    w_in_p, w_uq_p, w_k_p, w_vt = _pack_weights(w_in[0], w_uq[0], w_ukv[0])
    qa, kva, gate, qb, kb, vbt = _input_proj(
        x2, norm_mix, w_in_p, q_a_norm, w_uq_p, kv_a_norm, w_k_p, w_vt, cq, sq, ck, sk, seq)
    sink_tab = jnp.repeat(attn_sink[0].astype(jnp.float32) * LOG2E, BLOCK).reshape(A_KV_HEADS, A_REP * BLOCK)
    yb = _latent_attn(qb, kb, vbt, gate, batch, seq)
    w_o = w_out[0].astype(jnp.bfloat16)
    out = _window_attn_out(qa, kva, gate, bias, sink_tab, yb, x2,
                           w_o[:A_WIDTH][A_PAIR_COLS], w_o[A_WIDTH:], final_norm[None], batch, seq)
    return out.reshape(batch, seq, D_MODEL)
```

```python
import functools

import jax
import jax.numpy as jnp
import numpy as np
from jax.experimental import pallas as pl
from jax.experimental.pallas import tpu as pltpu

D_MODEL = 1024
HEAD_DIM = 64
BLOCK = 128
A_HEADS = 8
A_KV_HEADS = 2
A_REP = A_HEADS // A_KV_HEADS
A_WIDTH = A_HEADS * HEAD_DIM
WINDOW = 128
B_HEADS = 8
B_NOPE = 64
B_ROPE = 32
B_V = 64
B_WIDTH = B_HEADS * B_V
Q_LORA = 384
KV_LORA = 256
ROPE_BASE = 10000.0
EPS = 1e-6

LANES = 128
HALF_ROPE = B_ROPE // 2
NEG = -1e30
LOG2E = float(np.log2(np.e))
QA_SCALE = HEAD_DIM ** -0.5 * LOG2E
QB_SCALE = (B_NOPE + B_ROPE) ** -0.5 * LOG2E
VMEM_LIMIT = 56 * 1024 * 1024

NOPE_A = 48

A_PAIR_COLS = np.concatenate(
    [np.arange(HEAD_DIM) + (g * A_REP + p) * HEAD_DIM for p in range(A_REP) for g in range(A_KV_HEADS)])

C_QA = 0
C_KV = C_QA + A_WIDTH
C_GATE = C_KV + 2 * A_KV_HEADS * HEAD_DIM
C_CQ = C_GATE + A_WIDTH + B_WIDTH
C_CKV = C_CQ + Q_LORA
C_KR = C_CKV + KV_LORA
C_END = C_KR + LANES

TM_PROJ = 512
QB_WIN = 4
TQ_LAT = 512
CK_LAT = 256


def _rms(x, gain):
    return x * jax.lax.rsqrt(jnp.mean(x * x, axis=-1, keepdims=True) + EPS) * gain


def _rope(slab, cos, sin):
    return slab * cos + pltpu.roll(slab, 64, 1) * sin


def _proj_kernel(x_ref, g_ref, win_ref, qg_ref, wuq_ref, kvg_ref, wk_ref, wvt_ref,
                 cq_ref, sq_ref, ck_ref, sk_ref,
                 qa_ref, kva_ref, gate_ref, qb_ref, kb_ref, vbt_ref):
    h = _rms(x_ref[...], g_ref[...]).astype(jnp.bfloat16)

    def seg(lo, hi):
        return jnp.dot(h, win_ref[:, lo:hi], preferred_element_type=jnp.float32)

    qa_ref[...] = (seg(C_QA, C_KV) * QA_SCALE).astype(jnp.bfloat16)
    kva_ref[...] = seg(C_KV, C_GATE).astype(jnp.bfloat16)
    g = seg(C_GATE, C_CQ)
    gate_ref[...] = (g / (1.0 + jnp.exp(-g))).astype(jnp.bfloat16)

    cqn = _rms(seg(C_CQ, C_CKV), qg_ref[...]).astype(jnp.bfloat16)
    q = jnp.dot(cqn, wuq_ref[...], preferred_element_type=jnp.float32)
    cq, sq = cq_ref[...], sq_ref[...]
    for hd in range(B_HEADS):
        sl = slice(hd * LANES, (hd + 1) * LANES)
        qb_ref[:, sl] = _rope(q[:, sl], cq, sq).astype(jnp.bfloat16)

    ckvn = _rms(seg(C_CKV, C_KR), kvg_ref[...]).astype(jnp.bfloat16)
    kr = _rope(seg(C_KR, C_END), ck_ref[...], sk_ref[...])
    kn = jnp.dot(ckvn, wk_ref[...], preferred_element_type=jnp.float32)
    for hd in range(B_HEADS):
        sl = slice(hd * LANES, (hd + 1) * LANES)
        kb_ref[:, sl] = (kn[:, sl] + kr).astype(jnp.bfloat16)
    vbt_ref[...] = jax.lax.dot_general(wvt_ref[...], ckvn, (((1,), (1,)), ((), ())),
                                       preferred_element_type=jnp.float32).astype(jnp.bfloat16)


def _input_proj(x2, norm_g, w_in_p, q_g, w_uq_p, kv_g, w_k_p, w_vt, cq, sq, ck, sk, seq):
    t = x2.shape[0]
    tm = TM_PROJ
    n_pos = seq // tm
    full = lambda a: pl.BlockSpec(a.shape, lambda i: (0,) * a.ndim)
    rows = lambda w: pl.BlockSpec((tm, w), lambda i: (i, 0))
    tab = pl.BlockSpec((tm, LANES), lambda i: (i % n_pos, 0))
    bf = jnp.bfloat16
    widths = (A_WIDTH, 2 * A_KV_HEADS * HEAD_DIM, A_WIDTH + B_WIDTH,
              B_HEADS * LANES, B_HEADS * LANES)
    return pl.pallas_call(
        _proj_kernel,
        grid=(t // tm,),
        in_specs=[rows(D_MODEL), full(norm_g), full(w_in_p), full(q_g), full(w_uq_p),
                  full(kv_g), full(w_k_p), full(w_vt), tab, tab, tab, tab],
        out_specs=[rows(w) for w in widths]
        + [pl.BlockSpec((None, B_WIDTH, tm), lambda i: (i // n_pos, 0, i % n_pos))],
        out_shape=[jax.ShapeDtypeStruct((t, w), bf) for w in widths]
        + [jax.ShapeDtypeStruct((t // seq, B_WIDTH, seq), bf)],
        compiler_params=pltpu.CompilerParams(
            dimension_semantics=("arbitrary",), vmem_limit_bytes=VMEM_LIMIT),
        name="input_proj",
    )(x2, norm_g, w_in_p, q_g, w_uq_p, kv_g, w_k_p, w_vt, cq, sq, ck, sk)


def _window_block(q_ref, kva_ref, vt_ref, gate_ref, bias_ref, sink_ref, o_ref,
                  rows, n, bias_idx, n_blocks, between=()):
    band = (jnp.maximum(n - 1, 0), n, jnp.minimum(n + 1, n_blocks - 1))
    kk = jnp.concatenate(
        [kva_ref[pl.ds(pl.multiple_of(b * BLOCK, BLOCK), BLOCK), :LANES] for b in band],
        axis=0)
    vts = [vt_ref[b] for b in band]
    first = jax.lax.broadcasted_iota(jnp.int32, (BLOCK, LANES), 1) < HEAD_DIM
    zero = jnp.zeros((), jnp.bfloat16)
    qp = [q_ref[rows, p * LANES:(p + 1) * LANES] for p in range(A_REP)]

    outs = []
    for g in range(A_KV_HEADS):
        keep = first if g == 0 else jnp.logical_not(first)
        q4 = jnp.concatenate([jnp.where(keep, x, zero) for x in qp], axis=0)
        st = jax.lax.dot_general(kk, q4, (((1,), (1,)), ((), ())),
                                 preferred_element_type=jnp.float32)
        st = st + bias_ref[bias_idx, g]
        sink = sink_ref[g:g + 1, :]
        m = jnp.maximum(jnp.max(st, axis=0, keepdims=True), sink)
        p = jnp.exp2(st - m)
        inv = 1.0 / (jnp.sum(p, axis=0, keepdims=True) + jnp.exp2(sink - m))
        vt = jnp.concatenate([x[g * HEAD_DIM:(g + 1) * HEAD_DIM, :] for x in vts], axis=1)
        ot = jnp.dot(vt, p.astype(jnp.bfloat16), preferred_element_type=jnp.float32)
        outs.append(ot * inv)
        if g < len(between):
            between[g]()

    for pr in range(A_REP):
        sl = slice(pr * LANES, (pr + 1) * LANES)
        o = jnp.concatenate([outs[0][:, sl], outs[1][:, sl]], axis=0).T
        o_ref[rows, sl] = (o * gate_ref[rows, sl].astype(jnp.float32)).astype(jnp.bfloat16)


def _window_out_kernel(q_ref, kva_ref, gate_ref, bias_ref, sink_ref, yb_ref, x_ref, wa_ref, wb_ref,
                       g_ref, o_ref, vt_ref, ya_ref, *, n_blocks):
    step = pl.program_id(1)
    n_steps = n_blocks // QB_WIN

    @pl.when(step == 0)
    def _():
        for b in range(n_blocks):
            v = kva_ref[b * BLOCK:(b + 1) * BLOCK, LANES:].astype(jnp.float32)
            vt_ref[b] = v.T.astype(jnp.bfloat16)

    def out_proj(rows, half):
        def emit():
            cols = slice(half * (D_MODEL // 2), (half + 1) * (D_MODEL // 2))
            d = jnp.dot(ya_ref[rows, :], wa_ref[:, cols], preferred_element_type=jnp.float32)
            d = d + jnp.dot(yb_ref[rows, :], wb_ref[:, cols], preferred_element_type=jnp.float32)
            o_ref[rows, cols] = x_ref[rows, cols] + d
        return emit

    def finish(rows):
        def emit():
            out_proj(rows, 1)()
            o_ref[rows, :] = _rms(o_ref[rows, :], g_ref[...])
        return emit

    pending = ()
    for blk in range(QB_WIN):
        if blk == 0:
            bias_idx = jnp.where(step == 0, 0, 1)
        elif blk == QB_WIN - 1:
            bias_idx = jnp.where(step == n_steps - 1, 2, 1)
        else:
            bias_idx = 1
        rows = pl.ds(blk * BLOCK, BLOCK)
        _window_block(q_ref, kva_ref, vt_ref, gate_ref, bias_ref, sink_ref, ya_ref,
                      rows, step * QB_WIN + blk, bias_idx, n_blocks, between=pending)
        pending = (out_proj(rows, 0), finish(rows))
    for emit in pending:
        emit()


def _window_attn_out(qa, kva, gate, bias, sink_tab, yb, x2, w_a, w_b, final_g, batch, seq):
    nb = seq // BLOCK
    ns = nb // QB_WIN
    tq = QB_WIN * BLOCK
    rows = lambda w: pl.BlockSpec((tq, w), lambda b, n: (b * ns + n, 0))
    full = lambda a: pl.BlockSpec(a.shape, lambda b, n: (0,) * a.ndim)
    return pl.pallas_call(
        functools.partial(_window_out_kernel, n_blocks=nb),
        grid=(batch, ns),
        in_specs=[rows(A_WIDTH),
                  pl.BlockSpec((seq, 2 * A_KV_HEADS * HEAD_DIM), lambda b, n: (b, 0)),
                  rows(A_WIDTH), full(bias), full(sink_tab), rows(B_WIDTH), rows(D_MODEL),
                  full(w_a), full(w_b), full(final_g)],
        out_specs=rows(D_MODEL),
        out_shape=jax.ShapeDtypeStruct((batch * seq, D_MODEL), jnp.float32),
        scratch_shapes=[pltpu.VMEM((nb, LANES, BLOCK), jnp.bfloat16),
                        pltpu.VMEM((tq, A_WIDTH), jnp.bfloat16)],
        compiler_params=pltpu.CompilerParams(
            dimension_semantics=("arbitrary", "arbitrary"), vmem_limit_bytes=VMEM_LIMIT),
        name="window_attn_out",
    )(qa, kva, gate, bias, sink_tab, yb, x2, w_a, w_b, final_g)


def _latent_kernel(q_ref, k_ref, vt_ref, gate_ref, o_ref):
    n_chunks = k_ref.shape[0] // CK_LAT

    def scores(hd):
        sl = slice(hd * LANES, (hd + 1) * LANES)
        st = jax.lax.dot_general(k_ref[:, sl], q_ref[:, sl], (((1,), (1,)), ((), ())),
                                 preferred_element_type=jnp.float32)
        return st, jnp.max(st, axis=0, keepdims=True)

    outs = []
    nxt = scores(0)
    for hd in range(B_HEADS):
        st, m = nxt
        if hd + 1 < B_HEADS:
            nxt = scores(hd + 1)
        l = acc = None
        for c in range(n_chunks):
            keys = slice(c * CK_LAT, (c + 1) * CK_LAT)
            p = jnp.exp2(st[keys] - m)
            lc = jnp.sum(p, axis=0, keepdims=True)
            ac = jnp.dot(vt_ref[hd * B_V:(hd + 1) * B_V, keys], p.astype(jnp.bfloat16),
                         preferred_element_type=jnp.float32)
            l = lc if c == 0 else l + lc
            acc = ac if c == 0 else acc + ac
        outs.append(acc * (1.0 / l))
        if hd % 2 == 1:
            o = jnp.concatenate(outs, axis=0).T
            outs = []
            sl = slice((hd // 2) * LANES, (hd // 2 + 1) * LANES)
            o_ref[:, sl] = (o * gate_ref[:, sl].astype(jnp.float32)).astype(jnp.bfloat16)


def _latent_attn(qb, kb, vbt, gate, batch, seq):
    tq = TQ_LAT
    nq = seq // tq
    return pl.pallas_call(
        _latent_kernel,
        grid=(batch, nq),
        in_specs=[pl.BlockSpec((tq, B_HEADS * LANES), lambda b, i: (b * nq + i, 0)),
                  pl.BlockSpec((seq, B_HEADS * LANES), lambda b, i: (b, 0)),
                  pl.BlockSpec((None, B_WIDTH, seq), lambda b, i: (b, 0, 0)),
                  pl.BlockSpec((tq, B_WIDTH), lambda b, i: (b * nq + i, 1))],
        out_specs=pl.BlockSpec((tq, B_WIDTH), lambda b, i: (b * nq + i, 0)),
        out_shape=jax.ShapeDtypeStruct((batch * seq, B_WIDTH), jnp.bfloat16),
        compiler_params=pltpu.CompilerParams(
            dimension_semantics=("arbitrary", "arbitrary"), vmem_limit_bytes=VMEM_LIMIT),
        name="latent_attn",
    )(qb, kb, vbt, gate)


def _head_slab(x1, nope, x2):
    z = jnp.zeros((nope.shape[0], LANES - B_NOPE - B_ROPE), nope.dtype)
    return jnp.concatenate([x1, nope[:, :NOPE_A], x2, nope[:, NOPE_A:], z], axis=1)


def _pack_weights(w_in, w_uq, w_ukv):
    sizes = (A_WIDTH, A_KV_HEADS * HEAD_DIM, A_KV_HEADS * HEAD_DIM, A_WIDTH,
             Q_LORA, KV_LORA, B_ROPE, B_WIDTH)
    offs = np.cumsum((0,) + sizes)
    qa, ka, va, ga, cq, ckv, kr, gb = (w_in[:, offs[i]:offs[i + 1]] for i in range(8))
    zr = jnp.zeros((w_in.shape[0], B_NOPE), w_in.dtype)
    kr_slab = _head_slab(kr[:, :HALF_ROPE], zr, kr[:, HALF_ROPE:])
    w_in_p = jnp.concatenate(
        [qa[:, A_PAIR_COLS], ka, va, ga[:, A_PAIR_COLS], gb, cq, ckv, kr_slab], axis=1)

    per_q = B_NOPE + B_ROPE
    uq_slabs, k_slabs, v_cols = [], [], []
    zq = jnp.zeros((KV_LORA, HALF_ROPE), w_ukv.dtype)
    for hd in range(B_HEADS):
        wq = w_uq[:, hd * per_q:(hd + 1) * per_q]
        uq_slabs.append(_head_slab(wq[:, B_NOPE:B_NOPE + HALF_ROPE], wq[:, :B_NOPE],
                                   wq[:, B_NOPE + HALF_ROPE:]))
        wkv = w_ukv[:, hd * (B_NOPE + B_V):(hd + 1) * (B_NOPE + B_V)]
        k_slabs.append(_head_slab(zq, wkv[:, :B_NOPE], zq))
        v_cols.append(wkv[:, B_NOPE:])
    bf = jnp.bfloat16
    return (w_in_p.astype(bf), jnp.concatenate(uq_slabs, axis=1).astype(bf),
            jnp.concatenate(k_slabs, axis=1).astype(bf), jnp.concatenate(v_cols, axis=1).T.astype(bf))


def _rope_tables(seq):
    pos = jnp.arange(seq, dtype=jnp.float32)
    inv_freq = ROPE_BASE ** (-jnp.arange(0, B_ROPE, 2, dtype=jnp.float32) / B_ROPE)
    ang = pos[:, None] * inv_freq[None, :]
    cos, sin = jnp.cos(ang), jnp.sin(ang)
    one = jnp.ones((seq, B_NOPE), jnp.float32)
    zero = jnp.zeros((seq, B_NOPE), jnp.float32)
    c = _head_slab(cos, one, cos)
    s = _head_slab(-sin, zero, sin)
    return c * QB_SCALE, s * QB_SCALE, c, s


def _window_bias():
    kj = np.arange(3 * BLOCK)[:, None]
    qi = np.arange(BLOCK)[None, :]
    dist = np.abs(qi - kj + BLOCK).astype(np.float32)
    slopes = np.exp2(-8.0 * np.arange(1, A_HEADS + 1, dtype=np.float32) / A_HEADS)
    bias = -slopes[:, None, None] * dist[None] * LOG2E
    bias = np.where((dist <= WINDOW)[None], bias, NEG).astype(np.float32)
    bias = bias.reshape(A_KV_HEADS, A_REP, 3 * BLOCK, BLOCK).transpose(0, 2, 1, 3)
    bias = bias.reshape(A_KV_HEADS, 3 * BLOCK, A_REP * BLOCK)
    first, last = bias.copy(), bias.copy()
    first[:, :BLOCK] = NEG
    last[:, 2 * BLOCK:] = NEG
    return jnp.asarray(np.stack([first, bias, last]))


def kernel(x, norm_mix, w_in, attn_sink, q_a_norm, w_uq, kv_a_norm, w_ukv, w_out, final_norm):
    batch, seq, _ = x.shape
    assert norm_mix.shape[0] == 1, "single-layer block: the final norm is fused into the layer"
    assert seq % TM_PROJ == 0 and seq % TQ_LAT == 0 and seq % CK_LAT == 0
    assert seq % (QB_WIN * BLOCK) == 0
    cq, sq, ck, sk = _rope_tables(seq)
    bias = _window_bias()
    x2 = x.reshape(batch * seq, D_MODEL)
    w_in_p, w_uq_p, w_k_p, w_vt = _pack_weights(w_in[0], w_uq[0], w_ukv[0])
    qa, kva, gate, qb, kb, vbt = _input_proj(
        x2, norm_mix, w_in_p, q_a_norm, w_uq_p, kv_a_norm, w_k_p, w_vt, cq, sq, ck, sk, seq)
    sink_tab = jnp.repeat(attn_sink[0].astype(jnp.float32) * LOG2E, BLOCK).reshape(A_KV_HEADS, A_REP * BLOCK)
    yb = _latent_attn(qb, kb, vbt, gate, batch, seq)
    w_o = w_out[0].astype(jnp.bfloat16)
    out = _window_attn_out(qa, kva, gate, bias, sink_tab, yb, x2,
                           w_o[:A_WIDTH][A_PAIR_COLS], w_o[A_WIDTH:], final_norm[None], batch, seq)
    return out.reshape(batch, seq, D_MODEL)
```

```python
import functools

import jax
import jax.numpy as jnp
import numpy as np
from jax.experimental import pallas as pl
from jax.experimental.pallas import tpu as pltpu

D_MODEL = 1024
HEAD_DIM = 64
BLOCK = 128
A_HEADS = 8
A_KV_HEADS = 2
A_REP = A_HEADS // A_KV_HEADS
A_WIDTH = A_HEADS * HEAD_DIM
WINDOW = 128
B_HEADS = 8
B_NOPE = 64
B_ROPE = 32
B_V = 64
B_WIDTH = B_HEADS * B_V
Q_LORA = 384
KV_LORA = 256
ROPE_BASE = 10000.0
EPS = 1e-6

LANES = 128
HALF_ROPE = B_ROPE // 2
NEG = -1e30
LOG2E = float(np.log2(np.e))
QA_SCALE = HEAD_DIM ** -0.5 * LOG2E
QB_SCALE = (B_NOPE + B_ROPE) ** -0.5 * LOG2E
VMEM_LIMIT = 56 * 1024 * 1024

NOPE_A = 48

A_PAIR_COLS = np.concatenate(
    [np.arange(HEAD_DIM) + (g * A_REP + p) * HEAD_DIM for p in range(A_REP) for g in range(A_KV_HEADS)])

C_QA = 0
C_KV = C_QA + A_WIDTH
C_GATE = C_KV + 2 * A_KV_HEADS * HEAD_DIM
C_CQ = C_GATE + A_WIDTH + B_WIDTH
C_CKV = C_CQ + Q_LORA
C_KR = C_CKV + KV_LORA
C_END = C_KR + LANES

TM_PROJ = 512
QB_WIN = 4
TQ_LAT = 512
CK_LAT = 256


def _rms(x, gain):
    return x * jax.lax.rsqrt(jnp.mean(x * x, axis=-1, keepdims=True) + EPS) * gain


def _rope(slab, cos, sin):
    return slab * cos + pltpu.roll(slab, 64, 1) * sin


def _proj_kernel(x_ref, g_ref, win_ref, qg_ref, wuq_ref, kvg_ref, wk_ref, wvt_ref,
                 cq_ref, sq_ref, ck_ref, sk_ref,
                 qa_ref, kva_ref, gate_ref, qb_ref, kb_ref, vbt_ref):
    h = _rms(x_ref[...], g_ref[...]).astype(jnp.bfloat16)

    def seg(lo, hi):
        return jnp.dot(h, win_ref[:, lo:hi], preferred_element_type=jnp.float32)

    cqn = _rms(seg(C_CQ, C_CKV), qg_ref[...]).astype(jnp.bfloat16)
    ckvn = _rms(seg(C_CKV, C_KR), kvg_ref[...]).astype(jnp.bfloat16)
    kr = _rope(seg(C_KR, C_END), ck_ref[...], sk_ref[...])

    qa_ref[...] = (seg(C_QA, C_KV) * QA_SCALE).astype(jnp.bfloat16)
    kva_ref[...] = seg(C_KV, C_GATE).astype(jnp.bfloat16)

    q = jnp.dot(cqn, wuq_ref[...], preferred_element_type=jnp.float32)
    cq, sq = cq_ref[...], sq_ref[...]
    for hd in range(B_HEADS):
        sl = slice(hd * LANES, (hd + 1) * LANES)
        qb_ref[:, sl] = _rope(q[:, sl], cq, sq).astype(jnp.bfloat16)

    g = seg(C_GATE, C_CQ)
    gate_ref[...] = (g / (1.0 + jnp.exp(-g))).astype(jnp.bfloat16)

    kn = jnp.dot(ckvn, wk_ref[...], preferred_element_type=jnp.float32)
    for hd in range(B_HEADS):
        sl = slice(hd * LANES, (hd + 1) * LANES)
        kb_ref[:, sl] = (kn[:, sl] + kr).astype(jnp.bfloat16)
    vbt_ref[...] = jax.lax.dot_general(wvt_ref[...], ckvn, (((1,), (1,)), ((), ())),
                                       preferred_element_type=jnp.float32).astype(jnp.bfloat16)


def _input_proj(x2, norm_g, w_in_p, q_g, w_uq_p, kv_g, w_k_p, w_vt, cq, sq, ck, sk, seq):
    t = x2.shape[0]
    tm = TM_PROJ
    n_pos = seq // tm
    full = lambda a: pl.BlockSpec(a.shape, lambda i: (0,) * a.ndim)
    rows = lambda w: pl.BlockSpec((tm, w), lambda i: (i, 0))
    tab = pl.BlockSpec((tm, LANES), lambda i: (i % n_pos, 0))
    bf = jnp.bfloat16
    widths = (A_WIDTH, 2 * A_KV_HEADS * HEAD_DIM, A_WIDTH + B_WIDTH,
              B_HEADS * LANES, B_HEADS * LANES)
    return pl.pallas_call(
        _proj_kernel,
        grid=(t // tm,),
        in_specs=[rows(D_MODEL), full(norm_g), full(w_in_p), full(q_g), full(w_uq_p),
                  full(kv_g), full(w_k_p), full(w_vt), tab, tab, tab, tab],
        out_specs=[rows(w) for w in widths]
        + [pl.BlockSpec((None, B_WIDTH, tm), lambda i: (i // n_pos, 0, i % n_pos))],
        out_shape=[jax.ShapeDtypeStruct((t, w), bf) for w in widths]
        + [jax.ShapeDtypeStruct((t // seq, B_WIDTH, seq), bf)],
        compiler_params=pltpu.CompilerParams(
            dimension_semantics=("arbitrary",), vmem_limit_bytes=VMEM_LIMIT),
        name="input_proj",
    )(x2, norm_g, w_in_p, q_g, w_uq_p, kv_g, w_k_p, w_vt, cq, sq, ck, sk)


class _WindowStages:
    def __init__(self, refs, step, n_blocks):
        (self.q_ref, self.kva_ref, self.gate_ref, self.bias_ref, self.sink_ref, self.yb_ref,
         self.x_ref, self.wa_ref, self.wb_ref, self.g_ref, self.o_ref, self.vt_ref,
         self.ya_ref) = refs
        self.step = step
        self.n_blocks = n_blocks
        self.first = jax.lax.broadcasted_iota(jnp.int32, (BLOCK, LANES), 1) < HEAD_DIM

    def _band(self, blk):
        n = self.step * QB_WIN + blk
        return (jnp.maximum(n - 1, 0), n, jnp.minimum(n + 1, self.n_blocks - 1))

    def _bias_idx(self, blk):
        if blk == 0:
            return jnp.where(self.step == 0, 0, 1)
        if blk == QB_WIN - 1:
            return jnp.where(self.step == self.n_blocks // QB_WIN - 1, 2, 1)
        return 1

    def scores(self, blk):
        rows = pl.ds(blk * BLOCK, BLOCK)
        kk = jnp.concatenate(
            [self.kva_ref[pl.ds(pl.multiple_of(b * BLOCK, BLOCK), BLOCK), :LANES]
             for b in self._band(blk)], axis=0)
        qp = [self.q_ref[rows, p * LANES:(p + 1) * LANES] for p in range(A_REP)]
        zero = jnp.zeros((), jnp.bfloat16)
        out = []
        for g in range(A_KV_HEADS):
            keep = self.first if g == 0 else jnp.logical_not(self.first)
            q4 = jnp.concatenate([jnp.where(keep, x, zero) for x in qp], axis=0)
            st = jax.lax.dot_general(kk, q4, (((1,), (1,)), ((), ())),
                                     preferred_element_type=jnp.float32)
            out.append(st + self.bias_ref[self._bias_idx(blk), g])
        return out

    def softmax(self, sts):
        out = []
        for g, st in enumerate(sts):
            sink = self.sink_ref[g:g + 1, :]
            m = jnp.maximum(jnp.max(st, axis=0, keepdims=True), sink)
            p = jnp.exp2(st - m)
            inv = 1.0 / (jnp.sum(p, axis=0, keepdims=True) + jnp.exp2(sink - m))
            out.append((p.astype(jnp.bfloat16), inv))
        return out

    def values(self, blk, probs):
        rows = pl.ds(blk * BLOCK, BLOCK)
        vts = [self.vt_ref[b] for b in self._band(blk)]
        outs = []
        for g, (pb, inv) in enumerate(probs):
            vt = jnp.concatenate([x[g * HEAD_DIM:(g + 1) * HEAD_DIM, :] for x in vts], axis=1)
            outs.append(jnp.dot(vt, pb, preferred_element_type=jnp.float32) * inv)
        for pr in range(A_REP):
            sl = slice(pr * LANES, (pr + 1) * LANES)
            o = jnp.concatenate([outs[0][:, sl], outs[1][:, sl]], axis=0).T
            self.ya_ref[rows, sl] = (o * self.gate_ref[rows, sl].astype(jnp.float32)
                                     ).astype(jnp.bfloat16)

    def out_proj(self, rows):
        d = jnp.dot(self.ya_ref[rows, :], self.wa_ref[...], preferred_element_type=jnp.float32)
        d = d + jnp.dot(self.yb_ref[rows, :], self.wb_ref[...], preferred_element_type=jnp.float32)
        return self.x_ref[rows, :] + d

    def norm_store(self, rows, y):
        self.o_ref[rows, :] = _rms(y, self.g_ref[...])


def _window_out_kernel(*refs, n_blocks):
    kva_ref, vt_ref = refs[1], refs[11]
    step = pl.program_id(1)

    @pl.when(step == 0)
    def _():
        for b in range(n_blocks):
            v = kva_ref[b * BLOCK:(b + 1) * BLOCK, LANES:].astype(jnp.float32)
            vt_ref[b] = v.T.astype(jnp.bfloat16)

    st = _WindowStages(refs, step, n_blocks)
    halves = [list(range(0, QB_WIN // 2)), list(range(QB_WIN // 2, QB_WIN))]
    rows = [pl.ds(h[0] * BLOCK, len(h) * BLOCK) for h in halves]

    s_a = [st.scores(b) for b in halves[0]]
    s_b = [st.scores(b) for b in halves[1]]
    p_a = [st.softmax(s) for s in s_a]
    for b, p in zip(halves[0], p_a):
        st.values(b, p)
    p_b = [st.softmax(s) for s in s_b]
    y_a = st.out_proj(rows[0])
    for b, p in zip(halves[1], p_b):
        st.values(b, p)
    st.norm_store(rows[0], y_a)
    st.norm_store(rows[1], st.out_proj(rows[1]))


def _window_attn_out(qa, kva, gate, bias, sink_tab, yb, x2, w_a, w_b, final_g, batch, seq):
    nb = seq // BLOCK
    ns = nb // QB_WIN
    tq = QB_WIN * BLOCK
    rows = lambda w: pl.BlockSpec((tq, w), lambda b, n: (b * ns + n, 0))
    full = lambda a: pl.BlockSpec(a.shape, lambda b, n: (0,) * a.ndim)
    return pl.pallas_call(
        functools.partial(_window_out_kernel, n_blocks=nb),
        grid=(batch, ns),
        in_specs=[rows(A_WIDTH),
                  pl.BlockSpec((seq, 2 * A_KV_HEADS * HEAD_DIM), lambda b, n: (b, 0)),
                  rows(A_WIDTH), full(bias), full(sink_tab), rows(B_WIDTH), rows(D_MODEL),
                  full(w_a), full(w_b), full(final_g)],
        out_specs=rows(D_MODEL),
        out_shape=jax.ShapeDtypeStruct((batch * seq, D_MODEL), jnp.float32),
        scratch_shapes=[pltpu.VMEM((nb, LANES, BLOCK), jnp.bfloat16),
                        pltpu.VMEM((tq, A_WIDTH), jnp.bfloat16)],
        compiler_params=pltpu.CompilerParams(
            dimension_semantics=("arbitrary", "arbitrary"), vmem_limit_bytes=VMEM_LIMIT),
        name="window_attn_out",
    )(qa, kva, gate, bias, sink_tab, yb, x2, w_a, w_b, final_g)


def _latent_kernel(q_ref, k_ref, vt_ref, gate_ref, o_ref):
    n_chunks = k_ref.shape[0] // CK_LAT

    def scores(hd):
        sl = slice(hd * LANES, (hd + 1) * LANES)
        st = jax.lax.dot_general(k_ref[:, sl], q_ref[:, sl], (((1,), (1,)), ((), ())),
                                 preferred_element_type=jnp.float32)
        return st, jnp.max(st, axis=0, keepdims=True)

    outs = []
    nxt = scores(0)
    for hd in range(B_HEADS):
        st, m = nxt
        if hd + 1 < B_HEADS:
            nxt = scores(hd + 1)
        l = acc = None
        for c in range(n_chunks):
            keys = slice(c * CK_LAT, (c + 1) * CK_LAT)
            p = jnp.exp2(st[keys] - m)
            lc = jnp.sum(p, axis=0, keepdims=True)
            ac = jnp.dot(vt_ref[hd * B_V:(hd + 1) * B_V, keys], p.astype(jnp.bfloat16),
                         preferred_element_type=jnp.float32)
            l = lc if c == 0 else l + lc
            acc = ac if c == 0 else acc + ac
        outs.append(acc * (1.0 / l))
        if hd % 2 == 1:
            o = jnp.concatenate(outs, axis=0).T
            outs = []
            sl = slice((hd // 2) * LANES, (hd // 2 + 1) * LANES)
            o_ref[:, sl] = (o * gate_ref[:, sl].astype(jnp.float32)).astype(jnp.bfloat16)


def _latent_attn(qb, kb, vbt, gate, batch, seq):
    tq = TQ_LAT
    nq = seq // tq
    return pl.pallas_call(
        _latent_kernel,
        grid=(batch, nq),
        in_specs=[pl.BlockSpec((tq, B_HEADS * LANES), lambda b, i: (b * nq + i, 0)),
                  pl.BlockSpec((seq, B_HEADS * LANES), lambda b, i: (b, 0)),
                  pl.BlockSpec((None, B_WIDTH, seq), lambda b, i: (b, 0, 0)),
                  pl.BlockSpec((tq, B_WIDTH), lambda b, i: (b * nq + i, 1))],
        out_specs=pl.BlockSpec((tq, B_WIDTH), lambda b, i: (b * nq + i, 0)),
        out_shape=jax.ShapeDtypeStruct((batch * seq, B_WIDTH), jnp.bfloat16),
        compiler_params=pltpu.CompilerParams(
            dimension_semantics=("arbitrary", "arbitrary"), vmem_limit_bytes=VMEM_LIMIT),
        name="latent_attn",
    )(qb, kb, vbt, gate)


def _head_slab(x1, nope, x2):
    z = jnp.zeros((nope.shape[0], LANES - B_NOPE - B_ROPE), nope.dtype)
    return jnp.concatenate([x1, nope[:, :NOPE_A], x2, nope[:, NOPE_A:], z], axis=1)


def _pack_weights(w_in, w_uq, w_ukv):
    bf = jnp.bfloat16
    w_in, w_uq, w_ukv = w_in.astype(bf), w_uq.astype(bf), w_ukv.astype(bf)
    sizes = (A_WIDTH, A_KV_HEADS * HEAD_DIM, A_KV_HEADS * HEAD_DIM, A_WIDTH,
             Q_LORA, KV_LORA, B_ROPE, B_WIDTH)
    offs = np.cumsum((0,) + sizes)
    qa, ka, va, ga, cq, ckv, kr, gb = (w_in[:, offs[i]:offs[i + 1]] for i in range(8))
    zr = jnp.zeros((w_in.shape[0], B_NOPE), w_in.dtype)
    kr_slab = _head_slab(kr[:, :HALF_ROPE], zr, kr[:, HALF_ROPE:])
    w_in_p = jnp.concatenate(
        [qa[:, A_PAIR_COLS], ka, va, ga[:, A_PAIR_COLS], gb, cq, ckv, kr_slab], axis=1)

    per_q = B_NOPE + B_ROPE
    uq_slabs, k_slabs, v_cols = [], [], []
    zq = jnp.zeros((KV_LORA, HALF_ROPE), w_ukv.dtype)
    for hd in range(B_HEADS):
        wq = w_uq[:, hd * per_q:(hd + 1) * per_q]
        uq_slabs.append(_head_slab(wq[:, B_NOPE:B_NOPE + HALF_ROPE], wq[:, :B_NOPE],
                                   wq[:, B_NOPE + HALF_ROPE:]))
        wkv = w_ukv[:, hd * (B_NOPE + B_V):(hd + 1) * (B_NOPE + B_V)]
        k_slabs.append(_head_slab(zq, wkv[:, :B_NOPE], zq))
        v_cols.append(wkv[:, B_NOPE:])
    return (w_in_p, jnp.concatenate(uq_slabs, axis=1),
            jnp.concatenate(k_slabs, axis=1), jnp.concatenate(v_cols, axis=1).T)


def _rope_tables(seq):
    pos = np.arange(seq, dtype=np.float32)
    inv_freq = (np.float32(ROPE_BASE) ** (-np.arange(0, B_ROPE, 2, dtype=np.float32) / B_ROPE)
                ).astype(np.float32)
    ang = pos[:, None] * inv_freq[None, :]
    cos, sin = np.cos(ang), np.sin(ang)
    one = np.ones((seq, B_NOPE), np.float32)
    zero = np.zeros((seq, B_NOPE), np.float32)
    pad = np.zeros((seq, LANES - B_NOPE - B_ROPE), np.float32)
    c = np.concatenate([cos, one[:, :NOPE_A], cos, one[:, NOPE_A:], pad], axis=1)
    s = np.concatenate([-sin, zero[:, :NOPE_A], sin, zero[:, NOPE_A:], pad], axis=1)
    qs = np.float32(QB_SCALE)
    return tuple(jnp.asarray(t.astype(np.float32)) for t in (c * qs, s * qs, c, s))


def _window_bias():
    kj = np.arange(3 * BLOCK)[:, None]
    qi = np.arange(BLOCK)[None, :]
    dist = np.abs(qi - kj + BLOCK).astype(np.float32)
    slopes = np.exp2(-8.0 * np.arange(1, A_HEADS + 1, dtype=np.float32) / A_HEADS)
    bias = -slopes[:, None, None] * dist[None] * LOG2E
    bias = np.where((dist <= WINDOW)[None], bias, NEG).astype(np.float32)
    bias = bias.reshape(A_KV_HEADS, A_REP, 3 * BLOCK, BLOCK).transpose(0, 2, 1, 3)
    bias = bias.reshape(A_KV_HEADS, 3 * BLOCK, A_REP * BLOCK)
    first, last = bias.copy(), bias.copy()
    first[:, :BLOCK] = NEG
    last[:, 2 * BLOCK:] = NEG
    return jnp.asarray(np.stack([first, bias, last]))


def kernel(x, norm_mix, w_in, attn_sink, q_a_norm, w_uq, kv_a_norm, w_ukv, w_out, final_norm):
    batch, seq, _ = x.shape
    assert norm_mix.shape[0] == 1, "single-layer block: the final norm is fused into the layer"
    assert seq % TM_PROJ == 0 and seq % TQ_LAT == 0 and seq % CK_LAT == 0
    assert seq % (QB_WIN * BLOCK) == 0
    cq, sq, ck, sk = _rope_tables(seq)
    bias = _window_bias()
    x2 = x.reshape(batch * seq, D_MODEL)
    w_in_p, w_uq_p, w_k_p, w_vt = _pack_weights(w_in[0], w_uq[0], w_ukv[0])
    qa, kva, gate, qb, kb, vbt = _input_proj(
        x2, norm_mix, w_in_p, q_a_norm, w_uq_p, kv_a_norm, w_k_p, w_vt, cq, sq, ck, sk, seq)
    sink_tab = jnp.repeat(attn_sink[0].astype(jnp.float32) * LOG2E, BLOCK).reshape(A_KV_HEADS, A_REP * BLOCK)
    yb = _latent_attn(qb, kb, vbt, gate, batch, seq)
    w_o = w_out[0].astype(jnp.bfloat16)
    out = _window_attn_out(qa, kva, gate, bias, sink_tab, yb, x2,
                           w_o[:A_WIDTH][A_PAIR_COLS], w_o[A_WIDTH:], final_norm[None], batch, seq)
    return out.reshape(batch, seq, D_MODEL)
```

```python
import functools

import jax
import jax.numpy as jnp
import numpy as np
from jax.experimental import pallas as pl
from jax.experimental.pallas import tpu as pltpu

D_MODEL = 1024
HEAD_DIM = 64
BLOCK = 128
A_HEADS = 8
A_KV_HEADS = 2
A_REP = A_HEADS // A_KV_HEADS
A_WIDTH = A_HEADS * HEAD_DIM
WINDOW = 128
B_HEADS = 8
B_NOPE = 64
B_ROPE = 32
B_V = 64
B_WIDTH = B_HEADS * B_V
Q_LORA = 384
KV_LORA = 256
ROPE_BASE = 10000.0
EPS = 1e-6

LANES = 128
HALF_ROPE = B_ROPE // 2
NEG = -1e30
LOG2E = float(np.log2(np.e))
QA_SCALE = HEAD_DIM ** -0.5 * LOG2E
QB_SCALE = (B_NOPE + B_ROPE) ** -0.5 * LOG2E
VMEM_LIMIT = 56 * 1024 * 1024

NOPE_A = 48

A_PAIR_COLS = np.concatenate(
    [np.arange(HEAD_DIM) + (g * A_REP + p) * HEAD_DIM for p in range(A_REP) for g in range(A_KV_HEADS)])

C_QA = 0
C_KV = C_QA + A_WIDTH
C_GATE = C_KV + 2 * A_KV_HEADS * HEAD_DIM
C_CQ = C_GATE + A_WIDTH + B_WIDTH
C_CKV = C_CQ + Q_LORA
C_KR = C_CKV + KV_LORA
C_END = C_KR + LANES

TM_PROJ = 512
QB_WIN = 4
TQ_LAT = 512
CK_LAT = 256
BOUND_SLACK = 1.0 + 2.0 ** -10
L_FLOOR = 2.0 ** -80


def _rms(x, gain):
    return x * jax.lax.rsqrt(jnp.mean(x * x, axis=-1, keepdims=True) + EPS) * gain


def _sq_norm(slab_bf16):
    v = slab_bf16.astype(jnp.float32)
    return jnp.sum(v * v, axis=-1, keepdims=True)


def _rope(slab, cos, sin):
    return slab * cos + pltpu.roll(slab, 64, 1) * sin


def _proj_kernel(x_ref, g_ref, win_ref, qg_ref, wuq_ref, kvg_ref, wk_ref, wvt_ref,
                 cq_ref, sq_ref, ck_ref, sk_ref,
                 qa_ref, kva_ref, gate_ref, qb_ref, kb_ref, qn_ref, kn_ref, vbt_ref):
    h = _rms(x_ref[...], g_ref[...]).astype(jnp.bfloat16)

    def seg(lo, hi):
        return jnp.dot(h, win_ref[:, lo:hi], preferred_element_type=jnp.float32)

    cqn = _rms(seg(C_CQ, C_CKV), qg_ref[...]).astype(jnp.bfloat16)
    ckvn = _rms(seg(C_CKV, C_KR), kvg_ref[...]).astype(jnp.bfloat16)
    kr = _rope(seg(C_KR, C_END), ck_ref[...], sk_ref[...])

    qa_ref[...] = (seg(C_QA, C_KV) * QA_SCALE).astype(jnp.bfloat16)

    q = jnp.dot(cqn, wuq_ref[...], preferred_element_type=jnp.float32)
    cq, sq = cq_ref[...], sq_ref[...]
    lane = jax.lax.broadcasted_iota(jnp.int32, (x_ref.shape[0], LANES), 1)
    qn = jnp.zeros(lane.shape, jnp.float32)
    for hd in range(B_HEADS):
        sl = slice(hd * LANES, (hd + 1) * LANES)
        qh = _rope(q[:, sl], cq, sq).astype(jnp.bfloat16)
        qb_ref[:, sl] = qh
        qn = jnp.where(lane == hd, _sq_norm(qh), qn)
    qn_ref[...] = qn

    kva_ref[...] = seg(C_KV, C_GATE).astype(jnp.bfloat16)

    kn = jnp.dot(ckvn, wk_ref[...], preferred_element_type=jnp.float32)
    kn2 = jnp.zeros(lane.shape, jnp.float32)
    for hd in range(B_HEADS):
        sl = slice(hd * LANES, (hd + 1) * LANES)
        kh = (kn[:, sl] + kr).astype(jnp.bfloat16)
        kb_ref[:, sl] = kh
        kn2 = jnp.where(lane == hd, _sq_norm(kh), kn2)
    kn_ref[...] = kn2
    vbt_ref[...] = jax.lax.dot_general(wvt_ref[...], ckvn, (((1,), (1,)), ((), ())),
                                       preferred_element_type=jnp.float32).astype(jnp.bfloat16)

    g = seg(C_GATE, C_CQ)
    gate_ref[...] = (g / (1.0 + jnp.exp(-g))).astype(jnp.bfloat16)


def _input_proj(x2, norm_g, w_in_p, q_g, w_uq_p, kv_g, w_k_p, w_vt, cq, sq, ck, sk, seq):
    t = x2.shape[0]
    tm = TM_PROJ
    n_pos = seq // tm
    full = lambda a: pl.BlockSpec(a.shape, lambda i: (0,) * a.ndim)
    rows = lambda w: pl.BlockSpec((tm, w), lambda i: (i, 0))
    tab = pl.BlockSpec((tm, LANES), lambda i: (i % n_pos, 0))
    bf = jnp.bfloat16
    widths = (A_WIDTH, 2 * A_KV_HEADS * HEAD_DIM, A_WIDTH + B_WIDTH,
              B_HEADS * LANES, B_HEADS * LANES)
    return pl.pallas_call(
        _proj_kernel,
        grid=(t // tm,),
        in_specs=[rows(D_MODEL), full(norm_g), full(w_in_p), full(q_g), full(w_uq_p),
                  full(kv_g), full(w_k_p), full(w_vt), tab, tab, tab, tab],
        out_specs=[rows(w) for w in widths] + [rows(LANES), rows(LANES)]
        + [pl.BlockSpec((None, B_WIDTH, tm), lambda i: (i // n_pos, 0, i % n_pos))],
        out_shape=[jax.ShapeDtypeStruct((t, w), bf) for w in widths]
        + [jax.ShapeDtypeStruct((t, LANES), jnp.float32)] * 2
        + [jax.ShapeDtypeStruct((t // seq, B_WIDTH, seq), bf)],
        compiler_params=pltpu.CompilerParams(
            dimension_semantics=("arbitrary",), vmem_limit_bytes=VMEM_LIMIT),
        name="input_proj",
    )(x2, norm_g, w_in_p, q_g, w_uq_p, kv_g, w_k_p, w_vt, cq, sq, ck, sk)


class _WindowStages:
    def __init__(self, refs, step, n_blocks):
        (self.q_ref, self.kva_ref, self.gate_ref, self.bias_ref, self.sink_ref, self.yb_ref,
         self.x_ref, self.wa_ref, self.wb_ref, self.g_ref, self.o_ref, self.vt_ref,
         self.ya_ref) = refs
        self.step = step
        self.n_blocks = n_blocks
        self.first = jax.lax.broadcasted_iota(jnp.int32, (BLOCK, LANES), 1) < HEAD_DIM

    def _band(self, blk):
        n = self.step * QB_WIN + blk
        return (jnp.maximum(n - 1, 0), n, jnp.minimum(n + 1, self.n_blocks - 1))

    def _bias_idx(self, blk):
        if blk == 0:
            return jnp.where(self.step == 0, 0, 1)
        if blk == QB_WIN - 1:
            return jnp.where(self.step == self.n_blocks // QB_WIN - 1, 2, 1)
        return 1

    def scores(self, blk):
        rows = pl.ds(blk * BLOCK, BLOCK)
        kk = jnp.concatenate(
            [self.kva_ref[pl.ds(pl.multiple_of(b * BLOCK, BLOCK), BLOCK), :LANES]
             for b in self._band(blk)], axis=0)
        qp = [self.q_ref[rows, p * LANES:(p + 1) * LANES] for p in range(A_REP)]
        zero = jnp.zeros((), jnp.bfloat16)
        out = []
        for g in range(A_KV_HEADS):
            keep = self.first if g == 0 else jnp.logical_not(self.first)
            q4 = jnp.concatenate([jnp.where(keep, x, zero) for x in qp], axis=0)
            st = jax.lax.dot_general(kk, q4, (((1,), (1,)), ((), ())),
                                     preferred_element_type=jnp.float32)
            out.append(st + self.bias_ref[self._bias_idx(blk), g])
        return out

    def softmax(self, sts):
        out = []
        for g, st in enumerate(sts):
            sink = self.sink_ref[g:g + 1, :]
            m = jnp.maximum(jnp.max(st, axis=0, keepdims=True), sink)
            p = jnp.exp2(st - m)
            inv = 1.0 / (jnp.sum(p, axis=0, keepdims=True) + jnp.exp2(sink - m))
            out.append((p.astype(jnp.bfloat16), inv))
        return out

    def values(self, blk, probs):
        rows = pl.ds(blk * BLOCK, BLOCK)
        vts = [self.vt_ref[b] for b in self._band(blk)]
        outs = []
        for g, (pb, inv) in enumerate(probs):
            vt = jnp.concatenate([x[g * HEAD_DIM:(g + 1) * HEAD_DIM, :] for x in vts], axis=1)
            outs.append(jnp.dot(vt, pb, preferred_element_type=jnp.float32) * inv)
        for pr in range(A_REP):
            sl = slice(pr * LANES, (pr + 1) * LANES)
            o = jnp.concatenate([outs[0][:, sl], outs[1][:, sl]], axis=0).T
            self.ya_ref[rows, sl] = (o * self.gate_ref[rows, sl].astype(jnp.float32)
                                     ).astype(jnp.bfloat16)

    def out_proj(self, rows):
        d = jnp.dot(self.ya_ref[rows, :], self.wa_ref[...], preferred_element_type=jnp.float32)
        d = d + jnp.dot(self.yb_ref[rows, :], self.wb_ref[...], preferred_element_type=jnp.float32)
        return self.x_ref[rows, :] + d

    def norm_store(self, rows, y):
        self.o_ref[rows, :] = _rms(y, self.g_ref[...])


def _window_out_kernel(*refs, n_blocks):
    kva_ref, vt_ref = refs[1], refs[11]
    step = pl.program_id(1)

    @pl.when(step == 0)
    def _():
        for b in range(n_blocks):
            v = kva_ref[b * BLOCK:(b + 1) * BLOCK, LANES:].astype(jnp.float32)
            vt_ref[b] = v.T.astype(jnp.bfloat16)

    st = _WindowStages(refs, step, n_blocks)
    halves = [list(range(0, QB_WIN // 2)), list(range(QB_WIN // 2, QB_WIN))]
    rows = [pl.ds(h[0] * BLOCK, len(h) * BLOCK) for h in halves]

    s_a = [st.scores(b) for b in halves[0]]
    s_b = [st.scores(b) for b in halves[1]]
    p_a = [st.softmax(s) for s in s_a]
    for b, p in zip(halves[0], p_a):
        st.values(b, p)
    p_b = [st.softmax(s) for s in s_b]
    y_a = st.out_proj(rows[0])
    for b, p in zip(halves[1], p_b):
        st.values(b, p)
    st.norm_store(rows[0], y_a)
    st.norm_store(rows[1], st.out_proj(rows[1]))


def _window_attn_out(qa, kva, gate, bias, sink_tab, yb, x2, w_a, w_b, final_g, batch, seq):
    nb = seq // BLOCK
    ns = nb // QB_WIN
    tq = QB_WIN * BLOCK
    rows = lambda w: pl.BlockSpec((tq, w), lambda b, n: (b * ns + n, 0))
    full = lambda a: pl.BlockSpec(a.shape, lambda b, n: (0,) * a.ndim)
    return pl.pallas_call(
        functools.partial(_window_out_kernel, n_blocks=nb),
        grid=(batch, ns),
        in_specs=[rows(A_WIDTH),
                  pl.BlockSpec((seq, 2 * A_KV_HEADS * HEAD_DIM), lambda b, n: (b, 0)),
                  rows(A_WIDTH), full(bias), full(sink_tab), rows(B_WIDTH), rows(D_MODEL),
                  full(w_a), full(w_b), full(final_g)],
        out_specs=rows(D_MODEL),
        out_shape=jax.ShapeDtypeStruct((batch * seq, D_MODEL), jnp.float32),
        scratch_shapes=[pltpu.VMEM((nb, LANES, BLOCK), jnp.bfloat16),
                        pltpu.VMEM((tq, A_WIDTH), jnp.bfloat16)],
        compiler_params=pltpu.CompilerParams(
            dimension_semantics=("arbitrary", "arbitrary"), vmem_limit_bytes=VMEM_LIMIT),
        name="window_attn_out",
    )(qa, kva, gate, bias, sink_tab, yb, x2, w_a, w_b, final_g)


def _latent_kernel(q_ref, k_ref, vt_ref, gate_ref, qn_ref, kn_ref, o_ref):
    n_chunks = k_ref.shape[0] // CK_LAT

    def scores(hd):
        sl = slice(hd * LANES, (hd + 1) * LANES)
        return jax.lax.dot_general(k_ref[:, sl], q_ref[:, sl], (((1,), (1,)), ((), ())),
                                   preferred_element_type=jnp.float32)

    def weighted_values(hd, st, shift):
        l = acc = None
        for c in range(n_chunks):
            keys = slice(c * CK_LAT, (c + 1) * CK_LAT)
            p = jnp.exp2(st[keys] - shift)
            lc = jnp.sum(p, axis=0, keepdims=True)
            ac = jnp.dot(vt_ref[hd * B_V:(hd + 1) * B_V, keys], p.astype(jnp.bfloat16),
                         preferred_element_type=jnp.float32)
            l = lc if c == 0 else l + lc
            acc = ac if c == 0 else acc + ac
        return acc, l

    def store_pair(hd, outs):
        o = jnp.concatenate(outs, axis=0).T
        sl = slice((hd // 2) * LANES, (hd // 2 + 1) * LANES)
        o_ref[:, sl] = (o * gate_ref[:, sl].astype(jnp.float32)).astype(jnp.bfloat16)

    k_norm2 = jnp.max(kn_ref[...], axis=0, keepdims=True)
    q_norm2 = qn_ref[...].T
    outs, l_min = [], None
    for hd in range(B_HEADS):
        shift = jnp.sqrt(q_norm2[hd:hd + 1, :] * k_norm2[:, hd:hd + 1]) * BOUND_SLACK
        acc, l = weighted_values(hd, scores(hd), shift)
        l_min = l if hd == 0 else jnp.minimum(l_min, l)
        outs.append(acc * (1.0 / l))
        if hd % 2 == 1:
            store_pair(hd, outs)
            outs = []

    @pl.when(jnp.logical_not(jnp.min(l_min) >= L_FLOOR))
    def _():
        outs = []
        nxt = scores(0)
        for hd in range(B_HEADS):
            st = nxt
            if hd + 1 < B_HEADS:
                nxt = scores(hd + 1)
            acc, l = weighted_values(hd, st, jnp.max(st, axis=0, keepdims=True))
            outs.append(acc * (1.0 / l))
            if hd % 2 == 1:
                store_pair(hd, outs)
                outs = []


def _latent_attn(qb, kb, vbt, gate, qn, kn, batch, seq):
    tq = TQ_LAT
    nq = seq // tq
    return pl.pallas_call(
        _latent_kernel,
        grid=(batch, nq),
        in_specs=[pl.BlockSpec((tq, B_HEADS * LANES), lambda b, i: (b * nq + i, 0)),
                  pl.BlockSpec((seq, B_HEADS * LANES), lambda b, i: (b, 0)),
                  pl.BlockSpec((None, B_WIDTH, seq), lambda b, i: (b, 0, 0)),
                  pl.BlockSpec((tq, B_WIDTH), lambda b, i: (b * nq + i, 1)),
                  pl.BlockSpec((tq, LANES), lambda b, i: (b * nq + i, 0)),
                  pl.BlockSpec((seq, LANES), lambda b, i: (b, 0))],
        out_specs=pl.BlockSpec((tq, B_WIDTH), lambda b, i: (b * nq + i, 0)),
        out_shape=jax.ShapeDtypeStruct((batch * seq, B_WIDTH), jnp.bfloat16),
        compiler_params=pltpu.CompilerParams(
            dimension_semantics=("arbitrary", "arbitrary"), vmem_limit_bytes=VMEM_LIMIT),
        name="latent_attn",
    )(qb, kb, vbt, gate, qn, kn)


def _head_slab(x1, nope, x2):
    z = jnp.zeros((nope.shape[0], LANES - B_NOPE - B_ROPE), nope.dtype)
    return jnp.concatenate([x1, nope[:, :NOPE_A], x2, nope[:, NOPE_A:], z], axis=1)


def _pack_weights(w_in, w_uq, w_ukv):
    bf = jnp.bfloat16
    w_in, w_uq, w_ukv = w_in.astype(bf), w_uq.astype(bf), w_ukv.astype(bf)
    sizes = (A_WIDTH, A_KV_HEADS * HEAD_DIM, A_KV_HEADS * HEAD_DIM, A_WIDTH,
             Q_LORA, KV_LORA, B_ROPE, B_WIDTH)
    offs = np.cumsum((0,) + sizes)
    qa, ka, va, ga, cq, ckv, kr, gb = (w_in[:, offs[i]:offs[i + 1]] for i in range(8))
    zr = jnp.zeros((w_in.shape[0], B_NOPE), w_in.dtype)
    kr_slab = _head_slab(kr[:, :HALF_ROPE], zr, kr[:, HALF_ROPE:])
    w_in_p = jnp.concatenate(
        [qa[:, A_PAIR_COLS], ka, va, ga[:, A_PAIR_COLS], gb, cq, ckv, kr_slab], axis=1)

    per_q = B_NOPE + B_ROPE
    uq_slabs, k_slabs, v_cols = [], [], []
    zq = jnp.zeros((KV_LORA, HALF_ROPE), w_ukv.dtype)
    for hd in range(B_HEADS):
        wq = w_uq[:, hd * per_q:(hd + 1) * per_q]
        uq_slabs.append(_head_slab(wq[:, B_NOPE:B_NOPE + HALF_ROPE], wq[:, :B_NOPE],
                                   wq[:, B_NOPE + HALF_ROPE:]))
        wkv = w_ukv[:, hd * (B_NOPE + B_V):(hd + 1) * (B_NOPE + B_V)]
        k_slabs.append(_head_slab(zq, wkv[:, :B_NOPE], zq))
        v_cols.append(wkv[:, B_NOPE:])
    return (w_in_p, jnp.concatenate(uq_slabs, axis=1),
            jnp.concatenate(k_slabs, axis=1), jnp.concatenate(v_cols, axis=1).T)


def _rope_tables(seq):
    pos = np.arange(seq, dtype=np.float32)
    inv_freq = (np.float32(ROPE_BASE) ** (-np.arange(0, B_ROPE, 2, dtype=np.float32) / B_ROPE)
                ).astype(np.float32)
    ang = pos[:, None] * inv_freq[None, :]
    cos, sin = np.cos(ang), np.sin(ang)
    one = np.ones((seq, B_NOPE), np.float32)
    zero = np.zeros((seq, B_NOPE), np.float32)
    pad = np.zeros((seq, LANES - B_NOPE - B_ROPE), np.float32)
    c = np.concatenate([cos, one[:, :NOPE_A], cos, one[:, NOPE_A:], pad], axis=1)
    s = np.concatenate([-sin, zero[:, :NOPE_A], sin, zero[:, NOPE_A:], pad], axis=1)
    qs = np.float32(QB_SCALE)
    return tuple(jnp.asarray(t.astype(np.float32)) for t in (c * qs, s * qs, c, s))


def _window_bias():
    kj = np.arange(3 * BLOCK)[:, None]
    qi = np.arange(BLOCK)[None, :]
    dist = np.abs(qi - kj + BLOCK).astype(np.float32)
    slopes = np.exp2(-8.0 * np.arange(1, A_HEADS + 1, dtype=np.float32) / A_HEADS)
    bias = -slopes[:, None, None] * dist[None] * LOG2E
    bias = np.where((dist <= WINDOW)[None], bias, NEG).astype(np.float32)
    bias = bias.reshape(A_KV_HEADS, A_REP, 3 * BLOCK, BLOCK).transpose(0, 2, 1, 3)
    bias = bias.reshape(A_KV_HEADS, 3 * BLOCK, A_REP * BLOCK)
    first, last = bias.copy(), bias.copy()
    first[:, :BLOCK] = NEG
    last[:, 2 * BLOCK:] = NEG
    return jnp.asarray(np.stack([first, bias, last]))


def kernel(x, norm_mix, w_in, attn_sink, q_a_norm, w_uq, kv_a_norm, w_ukv, w_out, final_norm):
    batch, seq, _ = x.shape
    assert norm_mix.shape[0] == 1, "single-layer block: the final norm is fused into the layer"
    assert seq % TM_PROJ == 0 and seq % TQ_LAT == 0 and seq % CK_LAT == 0
    assert seq % (QB_WIN * BLOCK) == 0
    cq, sq, ck, sk = _rope_tables(seq)
    bias = _window_bias()
    x2 = x.reshape(batch * seq, D_MODEL)
    w_in_p, w_uq_p, w_k_p, w_vt = _pack_weights(w_in[0], w_uq[0], w_ukv[0])
    qa, kva, gate, qb, kb, qn, kn, vbt = _input_proj(
        x2, norm_mix, w_in_p, q_a_norm, w_uq_p, kv_a_norm, w_k_p, w_vt, cq, sq, ck, sk, seq)
    sink_tab = jnp.repeat(attn_sink[0].astype(jnp.float32) * LOG2E, BLOCK).reshape(A_KV_HEADS, A_REP * BLOCK)
    yb = _latent_attn(qb, kb, vbt, gate, qn, kn, batch, seq)
    w_o = w_out[0].astype(jnp.bfloat16)
    out = _window_attn_out(qa, kva, gate, bias, sink_tab, yb, x2,
                           w_o[:A_WIDTH][A_PAIR_COLS], w_o[A_WIDTH:], final_norm[None], batch, seq)
    return out.reshape(batch, seq, D_MODEL)
```

```python
import functools

import jax
import jax.numpy as jnp
import numpy as np
from jax.experimental import pallas as pl
from jax.experimental.pallas import tpu as pltpu

D_MODEL = 1024
HEAD_DIM = 64
BLOCK = 128
A_HEADS = 8
A_KV_HEADS = 2
A_REP = A_HEADS // A_KV_HEADS
A_WIDTH = A_HEADS * HEAD_DIM
WINDOW = 128
B_HEADS = 8
B_NOPE = 64
B_ROPE = 32
B_V = 64
B_WIDTH = B_HEADS * B_V
Q_LORA = 384
KV_LORA = 256
ROPE_BASE = 10000.0
EPS = 1e-6

LANES = 128
HALF_ROPE = B_ROPE // 2
NEG = -1e30
LOG2E = float(np.log2(np.e))
QA_SCALE = HEAD_DIM ** -0.5 * LOG2E
QB_SCALE = (B_NOPE + B_ROPE) ** -0.5 * LOG2E
VMEM_LIMIT = 56 * 1024 * 1024

NOPE_A = 48

A_PAIR_COLS = np.concatenate(
    [np.arange(HEAD_DIM) + (g * A_REP + p) * HEAD_DIM for p in range(A_REP) for g in range(A_KV_HEADS)])


def _slab_cols(x1, nope, x2, zero):
    return np.concatenate([x1, nope[:NOPE_A], x2, nope[NOPE_A:],
                           np.full(LANES - B_NOPE - B_ROPE, zero)])


PER_Q = B_NOPE + B_ROPE
PER_KV = B_NOPE + B_V
UQ_SLAB_COLS = np.concatenate([
    _slab_cols(h * PER_Q + B_NOPE + np.arange(HALF_ROPE), h * PER_Q + np.arange(B_NOPE),
               h * PER_Q + B_NOPE + HALF_ROPE + np.arange(HALF_ROPE), B_HEADS * PER_Q)
    for h in range(B_HEADS)])
UK_SLAB_COLS = np.concatenate([
    _slab_cols(np.full(HALF_ROPE, B_HEADS * PER_KV), h * PER_KV + np.arange(B_NOPE),
               np.full(HALF_ROPE, B_HEADS * PER_KV), B_HEADS * PER_KV)
    for h in range(B_HEADS)])
UV_COLS = np.concatenate([h * PER_KV + B_NOPE + np.arange(B_V) for h in range(B_HEADS)])

C_QA = 0
C_KV = C_QA + A_WIDTH
C_GATE = C_KV + 2 * A_KV_HEADS * HEAD_DIM
C_CQ = C_GATE + A_WIDTH + B_WIDTH
C_CKV = C_CQ + Q_LORA
C_KR = C_CKV + KV_LORA
C_END = C_KR + LANES

TM_PROJ = 512
QB_WIN = 8
GROUP_WIN = 2
TQ_LAT = 512
CK_LAT = 256
BOUND_SLACK = 1.0 + 2.0 ** -10
L_FLOOR = 2.0 ** -80


def _rms(x, gain):
    return x * jax.lax.rsqrt(jnp.mean(x * x, axis=-1, keepdims=True) + EPS) * gain


def _sq_norm(slab_bf16):
    v = slab_bf16.astype(jnp.float32)
    return jnp.sum(v * v, axis=-1, keepdims=True)


def _rope(slab, cos, sin):
    return slab * cos + pltpu.roll(slab, 64, 1) * sin


def _proj_kernel(x_ref, g_ref, win_ref, qg_ref, wuq_ref, kvg_ref, wk_ref, wvt_ref,
                 cq_ref, sq_ref, ck_ref, sk_ref,
                 qa_ref, kva_ref, gate_ref, qb_ref, kb_ref, qn_ref, kn_ref, vbt_ref):
    h = _rms(x_ref[...], g_ref[...]).astype(jnp.bfloat16)

    def seg(lo, hi):
        return jnp.dot(h, win_ref[:, lo:hi], preferred_element_type=jnp.float32)

    cqn = _rms(seg(C_CQ, C_CKV), qg_ref[...]).astype(jnp.bfloat16)
    ckvn = _rms(seg(C_CKV, C_KR), kvg_ref[...]).astype(jnp.bfloat16)
    kr = _rope(seg(C_KR, C_END), ck_ref[...], sk_ref[...])

    qa_ref[...] = (seg(C_QA, C_KV) * QA_SCALE).astype(jnp.bfloat16)

    q = jnp.dot(cqn, wuq_ref[...], preferred_element_type=jnp.float32)
    cq, sq = cq_ref[...], sq_ref[...]
    lane = jax.lax.broadcasted_iota(jnp.int32, (x_ref.shape[0], LANES), 1)
    qn = jnp.zeros(lane.shape, jnp.float32)
    for hd in range(B_HEADS):
        sl = slice(hd * LANES, (hd + 1) * LANES)
        qh = _rope(q[:, sl], cq, sq).astype(jnp.bfloat16)
        qb_ref[:, sl] = qh
        qn = jnp.where(lane == hd, _sq_norm(qh), qn)
    qn_ref[...] = qn

    kva_ref[...] = seg(C_KV, C_GATE).astype(jnp.bfloat16)

    kn = jnp.dot(ckvn, wk_ref[...], preferred_element_type=jnp.float32)
    kn2 = jnp.zeros(lane.shape, jnp.float32)
    for hd in range(B_HEADS):
        sl = slice(hd * LANES, (hd + 1) * LANES)
        kh = (kn[:, sl] + kr).astype(jnp.bfloat16)
        kb_ref[:, sl] = kh
        kn2 = jnp.where(lane == hd, _sq_norm(kh), kn2)
    kn_ref[...] = kn2
    vbt_ref[...] = jax.lax.dot_general(wvt_ref[...], ckvn, (((1,), (1,)), ((), ())),
                                       preferred_element_type=jnp.float32).astype(jnp.bfloat16)

    g = seg(C_GATE, C_CQ)
    gate_ref[...] = (g / (1.0 + jnp.exp(-g))).astype(jnp.bfloat16)


def _input_proj(x2, norm_g, w_in_p, q_g, w_uq_p, kv_g, w_k_p, w_vt, cq, sq, ck, sk, seq):
    t = x2.shape[0]
    tm = TM_PROJ
    n_pos = seq // tm
    full = lambda a: pl.BlockSpec(a.shape, lambda i: (0,) * a.ndim)
    rows = lambda w: pl.BlockSpec((tm, w), lambda i: (i, 0))
    tab = pl.BlockSpec((tm, LANES), lambda i: (i % n_pos, 0))
    bf = jnp.bfloat16
    widths = (A_WIDTH, 2 * A_KV_HEADS * HEAD_DIM, A_WIDTH + B_WIDTH,
              B_HEADS * LANES, B_HEADS * LANES)
    return pl.pallas_call(
        _proj_kernel,
        grid=(t // tm,),
        in_specs=[rows(D_MODEL), full(norm_g), full(w_in_p), full(q_g), full(w_uq_p),
                  full(kv_g), full(w_k_p), full(w_vt), tab, tab, tab, tab],
        out_specs=[rows(w) for w in widths] + [rows(LANES), rows(LANES)]
        + [pl.BlockSpec((None, B_WIDTH, tm), lambda i: (i // n_pos, 0, i % n_pos))],
        out_shape=[jax.ShapeDtypeStruct((t, w), bf) for w in widths]
        + [jax.ShapeDtypeStruct((t, LANES), jnp.float32)] * 2
        + [jax.ShapeDtypeStruct((t // seq, B_WIDTH, seq), bf)],
        compiler_params=pltpu.CompilerParams(
            dimension_semantics=("arbitrary",), vmem_limit_bytes=VMEM_LIMIT),
        name="input_proj",
    )(x2, norm_g, w_in_p, q_g, w_uq_p, kv_g, w_k_p, w_vt, cq, sq, ck, sk)


class _WindowStages:
    def __init__(self, refs, step, n_blocks):
        (self.q_ref, self.kva_ref, self.gate_ref, self.bias_ref, self.sink_ref, self.yb_ref,
         self.x_ref, self.wa_ref, self.wb_ref, self.g_ref, self.o_ref, self.vt_ref,
         self.ya_ref) = refs
        self.step = step
        self.n_blocks = n_blocks
        self.first = jax.lax.broadcasted_iota(jnp.int32, (BLOCK, LANES), 1) < HEAD_DIM

    def _band(self, blk):
        n = self.step * QB_WIN + blk
        return (jnp.maximum(n - 1, 0), n, jnp.minimum(n + 1, self.n_blocks - 1))

    def _bias_idx(self, blk):
        if blk == 0:
            return jnp.where(self.step == 0, 0, 1)
        if blk == QB_WIN - 1:
            return jnp.where(self.step == self.n_blocks // QB_WIN - 1, 2, 1)
        return 1

    def scores(self, blk):
        rows = pl.ds(blk * BLOCK, BLOCK)
        kk = jnp.concatenate(
            [self.kva_ref[pl.ds(pl.multiple_of(b * BLOCK, BLOCK), BLOCK), :LANES]
             for b in self._band(blk)], axis=0)
        qp = [self.q_ref[rows, p * LANES:(p + 1) * LANES] for p in range(A_REP)]
        zero = jnp.zeros((), jnp.bfloat16)
        out = []
        for g in range(A_KV_HEADS):
            keep = self.first if g == 0 else jnp.logical_not(self.first)
            q4 = jnp.concatenate([jnp.where(keep, x, zero) for x in qp], axis=0)
            st = jax.lax.dot_general(kk, q4, (((1,), (1,)), ((), ())),
                                     preferred_element_type=jnp.float32)
            out.append(st + self.bias_ref[self._bias_idx(blk), g])
        return out

    def softmax(self, sts):
        out = []
        for g, st in enumerate(sts):
            sink = self.sink_ref[g:g + 1, :]
            m = jnp.maximum(jnp.max(st, axis=0, keepdims=True), sink)
            p = jnp.exp2(st - m)
            inv = 1.0 / (jnp.sum(p, axis=0, keepdims=True) + jnp.exp2(sink - m))
            out.append((p.astype(jnp.bfloat16), inv))
        return out

    def values(self, blk, probs):
        rows = pl.ds(blk * BLOCK, BLOCK)
        vts = [self.vt_ref[b] for b in self._band(blk)]
        outs = []
        for g, (pb, inv) in enumerate(probs):
            vt = jnp.concatenate([x[g * HEAD_DIM:(g + 1) * HEAD_DIM, :] for x in vts], axis=1)
            outs.append(jnp.dot(vt, pb, preferred_element_type=jnp.float32) * inv)
        for pr in range(A_REP):
            sl = slice(pr * LANES, (pr + 1) * LANES)
            o = jnp.concatenate([outs[0][:, sl], outs[1][:, sl]], axis=0).T
            self.ya_ref[rows, sl] = (o * self.gate_ref[rows, sl].astype(jnp.float32)
                                     ).astype(jnp.bfloat16)

    def out_proj(self, rows):
        d = jnp.dot(self.ya_ref[rows, :], self.wa_ref[...], preferred_element_type=jnp.float32)
        d = d + jnp.dot(self.yb_ref[rows, :], self.wb_ref[...], preferred_element_type=jnp.float32)
        return self.x_ref[rows, :] + d

    def norm_store(self, rows, y):
        self.o_ref[rows, :] = _rms(y, self.g_ref[...])


def _window_out_kernel(*refs, n_blocks):
    kva_ref, vt_ref = refs[1], refs[11]
    step = pl.program_id(1)

    @pl.when(step == 0)
    def _():
        for b in range(n_blocks):
            v = kva_ref[b * BLOCK:(b + 1) * BLOCK, LANES:].astype(jnp.float32)
            vt_ref[b] = v.T.astype(jnp.bfloat16)

    st = _WindowStages(refs, step, n_blocks)
    groups = [list(range(g, g + GROUP_WIN)) for g in range(0, QB_WIN, GROUP_WIN)]
    rows = [pl.ds(g[0] * BLOCK, len(g) * BLOCK) for g in groups]
    state = [None] * len(groups)
    stages = [
        lambda i: [st.scores(b) for b in groups[i]],
        lambda i: [st.softmax(s) for s in state[i]],
        lambda i: [st.values(b, p) for b, p in zip(groups[i], state[i])],
        lambda i: st.out_proj(rows[i]),
        lambda i: st.norm_store(rows[i], state[i]),
    ]
    for tick in range(len(groups) + len(stages) - 1):
        for s in reversed(range(len(stages))):
            i = tick - s
            if 0 <= i < len(groups):
                state[i] = stages[s](i)


def _window_attn_out(qa, kva, gate, bias, sink_tab, yb, x2, w_a, w_b, final_g, batch, seq):
    nb = seq // BLOCK
    ns = nb // QB_WIN
    tq = QB_WIN * BLOCK
    rows = lambda w: pl.BlockSpec((tq, w), lambda b, n: (b * ns + n, 0))
    full = lambda a: pl.BlockSpec(a.shape, lambda b, n: (0,) * a.ndim)
    return pl.pallas_call(
        functools.partial(_window_out_kernel, n_blocks=nb),
        grid=(batch, ns),
        in_specs=[rows(A_WIDTH),
                  pl.BlockSpec((seq, 2 * A_KV_HEADS * HEAD_DIM), lambda b, n: (b, 0)),
                  rows(A_WIDTH), full(bias), full(sink_tab), rows(B_WIDTH), rows(D_MODEL),
                  full(w_a), full(w_b), full(final_g)],
        out_specs=rows(D_MODEL),
        out_shape=jax.ShapeDtypeStruct((batch * seq, D_MODEL), jnp.float32),
        scratch_shapes=[pltpu.VMEM((nb, LANES, BLOCK), jnp.bfloat16),
                        pltpu.VMEM((tq, A_WIDTH), jnp.bfloat16)],
        compiler_params=pltpu.CompilerParams(
            dimension_semantics=("arbitrary", "arbitrary"), vmem_limit_bytes=VMEM_LIMIT),
        name="window_attn_out",
    )(qa, kva, gate, bias, sink_tab, yb, x2, w_a, w_b, final_g)


def _latent_kernel(q_ref, k_ref, vt_ref, gate_ref, qn_ref, kn_ref, o_ref):
    n_chunks = k_ref.shape[0] // CK_LAT

    def scores(hd):
        sl = slice(hd * LANES, (hd + 1) * LANES)
        return jax.lax.dot_general(k_ref[:, sl], q_ref[:, sl], (((1,), (1,)), ((), ())),
                                   preferred_element_type=jnp.float32)

    def weighted_values(hd, st, shift):
        l = acc = None
        for c in range(n_chunks):
            keys = slice(c * CK_LAT, (c + 1) * CK_LAT)
            p = jnp.exp2(st[keys] - shift)
            lc = jnp.sum(p, axis=0, keepdims=True)
            ac = jnp.dot(vt_ref[hd * B_V:(hd + 1) * B_V, keys], p.astype(jnp.bfloat16),
                         preferred_element_type=jnp.float32)
            l = lc if c == 0 else l + lc
            acc = ac if c == 0 else acc + ac
        return acc, l

    def store_pair(hd, outs):
        o = jnp.concatenate(outs, axis=0).T
        sl = slice((hd // 2) * LANES, (hd // 2 + 1) * LANES)
        o_ref[:, sl] = (o * gate_ref[:, sl].astype(jnp.float32)).astype(jnp.bfloat16)

    k_norm2 = jnp.max(kn_ref[...], axis=0, keepdims=True)
    q_norm2 = qn_ref[...].T
    outs, l_min = [], None
    for hd in range(B_HEADS):
        shift = jnp.sqrt(q_norm2[hd:hd + 1, :] * k_norm2[:, hd:hd + 1]) * BOUND_SLACK
        acc, l = weighted_values(hd, scores(hd), shift)
        l_min = l if hd == 0 else jnp.minimum(l_min, l)
        outs.append(acc * (1.0 / l))
        if hd % 2 == 1:
            store_pair(hd, outs)
            outs = []

    @pl.when(jnp.logical_not(jnp.min(l_min) >= L_FLOOR))
    def _():
        outs = []
        nxt = scores(0)
        for hd in range(B_HEADS):
            st = nxt
            if hd + 1 < B_HEADS:
                nxt = scores(hd + 1)
            acc, l = weighted_values(hd, st, jnp.max(st, axis=0, keepdims=True))
            outs.append(acc * (1.0 / l))
            if hd % 2 == 1:
                store_pair(hd, outs)
                outs = []


def _latent_attn(qb, kb, vbt, gate, qn, kn, batch, seq):
    tq = TQ_LAT
    nq = seq // tq
    return pl.pallas_call(
        _latent_kernel,
        grid=(batch, nq),
        in_specs=[pl.BlockSpec((tq, B_HEADS * LANES), lambda b, i: (b * nq + i, 0)),
                  pl.BlockSpec((seq, B_HEADS * LANES), lambda b, i: (b, 0)),
                  pl.BlockSpec((None, B_WIDTH, seq), lambda b, i: (b, 0, 0)),
                  pl.BlockSpec((tq, B_WIDTH), lambda b, i: (b * nq + i, 1)),
                  pl.BlockSpec((tq, LANES), lambda b, i: (b * nq + i, 0)),
                  pl.BlockSpec((seq, LANES), lambda b, i: (b, 0))],
        out_specs=pl.BlockSpec((tq, B_WIDTH), lambda b, i: (b * nq + i, 0)),
        out_shape=jax.ShapeDtypeStruct((batch * seq, B_WIDTH), jnp.bfloat16),
        compiler_params=pltpu.CompilerParams(
            dimension_semantics=("arbitrary", "arbitrary"), vmem_limit_bytes=VMEM_LIMIT),
        name="latent_attn",
    )(qb, kb, vbt, gate, qn, kn)


def _head_slab(x1, nope, x2):
    z = jnp.zeros((nope.shape[0], LANES - B_NOPE - B_ROPE), nope.dtype)
    return jnp.concatenate([x1, nope[:, :NOPE_A], x2, nope[:, NOPE_A:], z], axis=1)


def _pack_weights(w_in, w_uq, w_ukv):
    bf = jnp.bfloat16
    w_in, w_uq, w_ukv = w_in.astype(bf), w_uq.astype(bf), w_ukv.astype(bf)
    sizes = (A_WIDTH, A_KV_HEADS * HEAD_DIM, A_KV_HEADS * HEAD_DIM, A_WIDTH,
             Q_LORA, KV_LORA, B_ROPE, B_WIDTH)
    offs = np.cumsum((0,) + sizes)
    qa, ka, va, ga, cq, ckv, kr, gb = (w_in[:, offs[i]:offs[i + 1]] for i in range(8))
    zr = jnp.zeros((w_in.shape[0], B_NOPE), w_in.dtype)
    kr_slab = _head_slab(kr[:, :HALF_ROPE], zr, kr[:, HALF_ROPE:])
    w_in_p = jnp.concatenate(
        [qa[:, A_PAIR_COLS], ka, va, ga[:, A_PAIR_COLS], gb, cq, ckv, kr_slab], axis=1)

    pad = lambda w: jnp.pad(w, ((0, 0), (0, 1)))
    w_uq_p = jnp.take(pad(w_uq), UQ_SLAB_COLS, axis=1)
    w_ukv = pad(w_ukv)
    return (w_in_p, w_uq_p, jnp.take(w_ukv, UK_SLAB_COLS, axis=1),
            jnp.take(w_ukv, UV_COLS, axis=1).T)


def _rope_tables(seq):
    pos = np.arange(seq, dtype=np.float32)
    inv_freq = (np.float32(ROPE_BASE) ** (-np.arange(0, B_ROPE, 2, dtype=np.float32) / B_ROPE)
                ).astype(np.float32)
    ang = pos[:, None] * inv_freq[None, :]
    cos, sin = np.cos(ang), np.sin(ang)
    one = np.ones((seq, B_NOPE), np.float32)
    zero = np.zeros((seq, B_NOPE), np.float32)
    pad = np.zeros((seq, LANES - B_NOPE - B_ROPE), np.float32)
    c = np.concatenate([cos, one[:, :NOPE_A], cos, one[:, NOPE_A:], pad], axis=1)
    s = np.concatenate([-sin, zero[:, :NOPE_A], sin, zero[:, NOPE_A:], pad], axis=1)
    qs = np.float32(QB_SCALE)
    return tuple(jnp.asarray(t.astype(np.float32)) for t in (c * qs, s * qs, c, s))


def _window_bias():
    kj = np.arange(3 * BLOCK)[:, None]
    qi = np.arange(BLOCK)[None, :]
    dist = np.abs(qi - kj + BLOCK).astype(np.float32)
    slopes = np.exp2(-8.0 * np.arange(1, A_HEADS + 1, dtype=np.float32) / A_HEADS)
    bias = -slopes[:, None, None] * dist[None] * LOG2E
    bias = np.where((dist <= WINDOW)[None], bias, NEG).astype(np.float32)
    bias = bias.reshape(A_KV_HEADS, A_REP, 3 * BLOCK, BLOCK).transpose(0, 2, 1, 3)
    bias = bias.reshape(A_KV_HEADS, 3 * BLOCK, A_REP * BLOCK)
    first, last = bias.copy(), bias.copy()
    first[:, :BLOCK] = NEG
    last[:, 2 * BLOCK:] = NEG
    return jnp.asarray(np.stack([first, bias, last]))


def kernel(x, norm_mix, w_in, attn_sink, q_a_norm, w_uq, kv_a_norm, w_ukv, w_out, final_norm):
    batch, seq, _ = x.shape
    assert norm_mix.shape[0] == 1, "single-layer block: the final norm is fused into the layer"
    assert seq % TM_PROJ == 0 and seq % TQ_LAT == 0 and seq % CK_LAT == 0
    assert seq % (QB_WIN * BLOCK) == 0
    cq, sq, ck, sk = _rope_tables(seq)
    bias = _window_bias()
    x2 = x.reshape(batch * seq, D_MODEL)
    w_in_p, w_uq_p, w_k_p, w_vt = _pack_weights(w_in[0], w_uq[0], w_ukv[0])
    qa, kva, gate, qb, kb, qn, kn, vbt = _input_proj(
        x2, norm_mix, w_in_p, q_a_norm, w_uq_p, kv_a_norm, w_k_p, w_vt, cq, sq, ck, sk, seq)
    sink_tab = jnp.repeat(attn_sink[0].astype(jnp.float32) * LOG2E, BLOCK).reshape(A_KV_HEADS, A_REP * BLOCK)
    yb = _latent_attn(qb, kb, vbt, gate, qn, kn, batch, seq)
    w_o = w_out[0].astype(jnp.bfloat16)
    out = _window_attn_out(qa, kva, gate, bias, sink_tab, yb, x2,
                           w_o[:A_WIDTH][A_PAIR_COLS], w_o[A_WIDTH:], final_norm[None], batch, seq)
    return out.reshape(batch, seq, D_MODEL)
```

```python
import functools

import jax
import jax.numpy as jnp
import numpy as np
from jax.experimental import pallas as pl
from jax.experimental.pallas import tpu as pltpu

D_MODEL = 1024
HEAD_DIM = 64
BLOCK = 128
A_HEADS = 8
A_KV_HEADS = 2
A_REP = A_HEADS // A_KV_HEADS
A_WIDTH = A_HEADS * HEAD_DIM
WINDOW = 128
B_HEADS = 8
B_NOPE = 64
B_ROPE = 32
B_V = 64
B_WIDTH = B_HEADS * B_V
Q_LORA = 384
KV_LORA = 256
ROPE_BASE = 10000.0
EPS = 1e-6

LANES = 128
HALF_ROPE = B_ROPE // 2
NEG = -1e30
LOG2E = float(np.log2(np.e))
QA_SCALE = HEAD_DIM ** -0.5 * LOG2E
QB_SCALE = (B_NOPE + B_ROPE) ** -0.5 * LOG2E
VMEM_LIMIT = 56 * 1024 * 1024

NOPE_A = 48

A_PAIR_COLS = np.concatenate(
    [np.arange(HEAD_DIM) + (g * A_REP + p) * HEAD_DIM for p in range(A_REP) for g in range(A_KV_HEADS)])


def _slab_cols(x1, nope, x2, zero):
    return np.concatenate([x1, nope[:NOPE_A], x2, nope[NOPE_A:],
                           np.full(LANES - B_NOPE - B_ROPE, zero)])


PER_Q = B_NOPE + B_ROPE
PER_KV = B_NOPE + B_V
UQ_SLAB_COLS = np.concatenate([
    _slab_cols(h * PER_Q + B_NOPE + np.arange(HALF_ROPE), h * PER_Q + np.arange(B_NOPE),
               h * PER_Q + B_NOPE + HALF_ROPE + np.arange(HALF_ROPE), B_HEADS * PER_Q)
    for h in range(B_HEADS)])
UK_SLAB_COLS = np.concatenate([
    _slab_cols(np.full(HALF_ROPE, B_HEADS * PER_KV), h * PER_KV + np.arange(B_NOPE),
               np.full(HALF_ROPE, B_HEADS * PER_KV), B_HEADS * PER_KV)
    for h in range(B_HEADS)])
UV_COLS = np.concatenate([h * PER_KV + B_NOPE + np.arange(B_V) for h in range(B_HEADS)])

C_QA = 0
C_KV = C_QA + A_WIDTH
C_GATE = C_KV + 2 * A_KV_HEADS * HEAD_DIM
C_CQ = C_GATE + A_WIDTH + B_WIDTH
C_CKV = C_CQ + Q_LORA
C_KR = C_CKV + KV_LORA
C_END = C_KR + LANES

TM_PROJ = 512
QB_WIN = 8
GROUP_WIN = 2
TQ_LAT = 1024
CK_LAT = 256
BOUND_SLACK = 1.0 + 2.0 ** -10
L_FLOOR = 2.0 ** -80


def _rms(x, gain):
    return x * jax.lax.rsqrt(jnp.mean(x * x, axis=-1, keepdims=True) + EPS) * gain


def _sq_norm(slab_bf16):
    v = slab_bf16.astype(jnp.float32)
    return jnp.sum(v * v, axis=-1, keepdims=True)


def _rope(slab, cos, sin):
    return slab * cos + pltpu.roll(slab, 64, 1) * sin


def _proj_kernel(x_ref, g_ref, win_ref, qg_ref, wuq_ref, kvg_ref, wk_ref, wvt_ref,
                 cq_ref, sq_ref, ck_ref, sk_ref,
                 qa_ref, kva_ref, gate_ref, qb_ref, kb_ref, qn_ref, kn_ref, vbt_ref):
    h = _rms(x_ref[...], g_ref[...]).astype(jnp.bfloat16)

    def seg(lo, hi):
        return jnp.dot(h, win_ref[:, lo:hi], preferred_element_type=jnp.float32)

    cqn = _rms(seg(C_CQ, C_CKV), qg_ref[...]).astype(jnp.bfloat16)
    ckvn = _rms(seg(C_CKV, C_KR), kvg_ref[...]).astype(jnp.bfloat16)
    kr = _rope(seg(C_KR, C_END), ck_ref[...], sk_ref[...])

    qa_ref[...] = (seg(C_QA, C_KV) * QA_SCALE).astype(jnp.bfloat16)

    q = jnp.dot(cqn, wuq_ref[...], preferred_element_type=jnp.float32)
    cq, sq = cq_ref[...], sq_ref[...]
    lane = jax.lax.broadcasted_iota(jnp.int32, (x_ref.shape[0], LANES), 1)
    qn = jnp.zeros(lane.shape, jnp.float32)
    for hd in range(B_HEADS):
        sl = slice(hd * LANES, (hd + 1) * LANES)
        qh = _rope(q[:, sl], cq, sq).astype(jnp.bfloat16)
        qb_ref[:, sl] = qh
        qn = jnp.where(lane == hd, _sq_norm(qh), qn)
    qn_ref[...] = qn

    kva_ref[...] = seg(C_KV, C_GATE).astype(jnp.bfloat16)

    kn = jnp.dot(ckvn, wk_ref[...], preferred_element_type=jnp.float32)
    kn2 = jnp.zeros(lane.shape, jnp.float32)
    for hd in range(B_HEADS):
        sl = slice(hd * LANES, (hd + 1) * LANES)
        kh = (kn[:, sl] + kr).astype(jnp.bfloat16)
        kb_ref[:, sl] = kh
        kn2 = jnp.where(lane == hd, _sq_norm(kh), kn2)
    kn_ref[...] = kn2
    vbt_ref[...] = jax.lax.dot_general(wvt_ref[...], ckvn, (((1,), (1,)), ((), ())),
                                       preferred_element_type=jnp.float32).astype(jnp.bfloat16)

    g = seg(C_GATE, C_CQ)
    gate_ref[...] = (g / (1.0 + jnp.exp(-g))).astype(jnp.bfloat16)


def _input_proj(x2, norm_g, w_in_p, q_g, w_uq_p, kv_g, w_k_p, w_vt, cq, sq, ck, sk, seq):
    t = x2.shape[0]
    tm = TM_PROJ
    n_pos = seq // tm
    full = lambda a: pl.BlockSpec(a.shape, lambda i: (0,) * a.ndim)
    rows = lambda w: pl.BlockSpec((tm, w), lambda i: (i, 0))
    tab = pl.BlockSpec((tm, LANES), lambda i: (i % n_pos, 0))
    bf = jnp.bfloat16
    widths = (A_WIDTH, 2 * A_KV_HEADS * HEAD_DIM, A_WIDTH + B_WIDTH,
              B_HEADS * LANES, B_HEADS * LANES)
    return pl.pallas_call(
        _proj_kernel,
        grid=(t // tm,),
        in_specs=[rows(D_MODEL), full(norm_g), full(w_in_p), full(q_g), full(w_uq_p),
                  full(kv_g), full(w_k_p), full(w_vt), tab, tab, tab, tab],
        out_specs=[rows(w) for w in widths] + [rows(LANES), rows(LANES)]
        + [pl.BlockSpec((None, B_WIDTH, tm), lambda i: (i // n_pos, 0, i % n_pos))],
        out_shape=[jax.ShapeDtypeStruct((t, w), bf) for w in widths]
        + [jax.ShapeDtypeStruct((t, LANES), jnp.float32)] * 2
        + [jax.ShapeDtypeStruct((t // seq, B_WIDTH, seq), bf)],
        compiler_params=pltpu.CompilerParams(
            dimension_semantics=("arbitrary",), vmem_limit_bytes=VMEM_LIMIT),
        name="input_proj",
    )(x2, norm_g, w_in_p, q_g, w_uq_p, kv_g, w_k_p, w_vt, cq, sq, ck, sk)


class _WindowStages:
    def __init__(self, refs, step, n_blocks):
        (self.q_ref, self.kva_ref, self.gate_ref, self.bias_ref, self.sink_ref, self.yb_ref,
         self.x_ref, self.wa_ref, self.wb_ref, self.g_ref, self.o_ref, self.vt_ref,
         self.ya_ref) = refs
        self.step = step
        self.n_blocks = n_blocks
        self.first = jax.lax.broadcasted_iota(jnp.int32, (BLOCK, LANES), 1) < HEAD_DIM

    def _band(self, blk):
        n = self.step * QB_WIN + blk
        return (jnp.maximum(n - 1, 0), n, jnp.minimum(n + 1, self.n_blocks - 1))

    def _bias_idx(self, blk):
        if blk == 0:
            return jnp.where(self.step == 0, 0, 1)
        if blk == QB_WIN - 1:
            return jnp.where(self.step == self.n_blocks // QB_WIN - 1, 2, 1)
        return 1

    def scores(self, blk):
        rows = pl.ds(blk * BLOCK, BLOCK)
        kk = jnp.concatenate(
            [self.kva_ref[pl.ds(pl.multiple_of(b * BLOCK, BLOCK), BLOCK), :LANES]
             for b in self._band(blk)], axis=0)
        qp = [self.q_ref[rows, p * LANES:(p + 1) * LANES] for p in range(A_REP)]
        zero = jnp.zeros((), jnp.bfloat16)
        out = []
        for g in range(A_KV_HEADS):
            keep = self.first if g == 0 else jnp.logical_not(self.first)
            q4 = jnp.concatenate([jnp.where(keep, x, zero) for x in qp], axis=0)
            st = jax.lax.dot_general(kk, q4, (((1,), (1,)), ((), ())),
                                     preferred_element_type=jnp.float32)
            out.append(st + self.bias_ref[self._bias_idx(blk), g])
        return out

    def softmax(self, sts):
        out = []
        for g, st in enumerate(sts):
            sink = self.sink_ref[g:g + 1, :]
            m = jnp.maximum(jnp.max(st, axis=0, keepdims=True), sink)
            p = jnp.exp2(st - m)
            inv = 1.0 / (jnp.sum(p, axis=0, keepdims=True) + jnp.exp2(sink - m))
            out.append((p.astype(jnp.bfloat16), inv))
        return out

    def values(self, blk, probs):
        rows = pl.ds(blk * BLOCK, BLOCK)
        vts = [self.vt_ref[b] for b in self._band(blk)]
        outs = []
        for g, (pb, inv) in enumerate(probs):
            vt = jnp.concatenate([x[g * HEAD_DIM:(g + 1) * HEAD_DIM, :] for x in vts], axis=1)
            outs.append(jnp.dot(vt, pb, preferred_element_type=jnp.float32) * inv)
        for pr in range(A_REP):
            sl = slice(pr * LANES, (pr + 1) * LANES)
            o = jnp.concatenate([outs[0][:, sl], outs[1][:, sl]], axis=0).T
            self.ya_ref[rows, sl] = (o * self.gate_ref[rows, sl].astype(jnp.float32)
                                     ).astype(jnp.bfloat16)

    def out_proj(self, rows):
        d = jnp.dot(self.ya_ref[rows, :], self.wa_ref[...], preferred_element_type=jnp.float32)
        d = d + jnp.dot(self.yb_ref[rows, :], self.wb_ref[...], preferred_element_type=jnp.float32)
        return self.x_ref[rows, :] + d

    def norm_store(self, rows, y):
        self.o_ref[rows, :] = _rms(y, self.g_ref[...])


def _window_out_kernel(*refs, n_blocks):
    kva_ref, vt_ref = refs[1], refs[11]
    step = pl.program_id(1)

    @pl.when(step == 0)
    def _():
        for b in range(n_blocks):
            v = kva_ref[b * BLOCK:(b + 1) * BLOCK, LANES:].astype(jnp.float32)
            vt_ref[b] = v.T.astype(jnp.bfloat16)

    st = _WindowStages(refs, step, n_blocks)
    groups = [list(range(g, g + GROUP_WIN)) for g in range(0, QB_WIN, GROUP_WIN)]
    rows = [pl.ds(g[0] * BLOCK, len(g) * BLOCK) for g in groups]
    state = [None] * len(groups)
    stages = [
        lambda i: [st.scores(b) for b in groups[i]],
        lambda i: [st.softmax(s) for s in state[i]],
        lambda i: [st.values(b, p) for b, p in zip(groups[i], state[i])],
        lambda i: st.out_proj(rows[i]),
        lambda i: st.norm_store(rows[i], state[i]),
    ]
    for tick in range(len(groups) + len(stages) - 1):
        for s in reversed(range(len(stages))):
            i = tick - s
            if 0 <= i < len(groups):
                state[i] = stages[s](i)


def _window_attn_out(qa, kva, gate, bias, sink_tab, yb, x2, w_a, w_b, final_g, batch, seq):
    nb = seq // BLOCK
    ns = nb // QB_WIN
    tq = QB_WIN * BLOCK
    rows = lambda w: pl.BlockSpec((tq, w), lambda b, n: (b * ns + n, 0))
    full = lambda a: pl.BlockSpec(a.shape, lambda b, n: (0,) * a.ndim)
    return pl.pallas_call(
        functools.partial(_window_out_kernel, n_blocks=nb),
        grid=(batch, ns),
        in_specs=[rows(A_WIDTH),
                  pl.BlockSpec((seq, 2 * A_KV_HEADS * HEAD_DIM), lambda b, n: (b, 0)),
                  rows(A_WIDTH), full(bias), full(sink_tab), rows(B_WIDTH), rows(D_MODEL),
                  full(w_a), full(w_b), full(final_g)],
        out_specs=rows(D_MODEL),
        out_shape=jax.ShapeDtypeStruct((batch * seq, D_MODEL), jnp.float32),
        scratch_shapes=[pltpu.VMEM((nb, LANES, BLOCK), jnp.bfloat16),
                        pltpu.VMEM((tq, A_WIDTH), jnp.bfloat16)],
        compiler_params=pltpu.CompilerParams(
            dimension_semantics=("arbitrary", "arbitrary"), vmem_limit_bytes=VMEM_LIMIT),
        name="window_attn_out",
    )(qa, kva, gate, bias, sink_tab, yb, x2, w_a, w_b, final_g)


def _latent_kernel(q_ref, k_ref, vt_ref, gate_ref, qn_ref, kn_ref, o_ref):
    n_chunks = k_ref.shape[0] // CK_LAT

    def scores(hd):
        sl = slice(hd * LANES, (hd + 1) * LANES)
        return jax.lax.dot_general(k_ref[:, sl], q_ref[:, sl], (((1,), (1,)), ((), ())),
                                   preferred_element_type=jnp.float32)

    def weighted_values(hd, st, shift):
        l = acc = None
        for c in range(n_chunks):
            keys = slice(c * CK_LAT, (c + 1) * CK_LAT)
            p = jnp.exp2(st[keys] - shift)
            lc = jnp.sum(p, axis=0, keepdims=True)
            ac = jnp.dot(vt_ref[hd * B_V:(hd + 1) * B_V, keys], p.astype(jnp.bfloat16),
                         preferred_element_type=jnp.float32)
            l = lc if c == 0 else l + lc
            acc = ac if c == 0 else acc + ac
        return acc, l

    def store_pair(hd, outs):
        o = jnp.concatenate(outs, axis=0).T
        sl = slice((hd // 2) * LANES, (hd // 2 + 1) * LANES)
        o_ref[:, sl] = (o * gate_ref[:, sl].astype(jnp.float32)).astype(jnp.bfloat16)

    k_norm2 = jnp.max(kn_ref[...], axis=0, keepdims=True)
    q_norm2 = qn_ref[...].T
    outs, l_min = [], None
    for hd in range(B_HEADS):
        shift = jnp.sqrt(q_norm2[hd:hd + 1, :] * k_norm2[:, hd:hd + 1]) * BOUND_SLACK
        acc, l = weighted_values(hd, scores(hd), shift)
        l_min = l if hd == 0 else jnp.minimum(l_min, l)
        outs.append(acc * (1.0 / l))
        if hd % 2 == 1:
            store_pair(hd, outs)
            outs = []

    @pl.when(jnp.logical_not(jnp.min(l_min) >= L_FLOOR))
    def _():
        outs = []
        nxt = scores(0)
        for hd in range(B_HEADS):
            st = nxt
            if hd + 1 < B_HEADS:
                nxt = scores(hd + 1)
            acc, l = weighted_values(hd, st, jnp.max(st, axis=0, keepdims=True))
            outs.append(acc * (1.0 / l))
            if hd % 2 == 1:
                store_pair(hd, outs)
                outs = []


def _latent_attn(qb, kb, vbt, gate, qn, kn, batch, seq):
    tq = TQ_LAT
    nq = seq // tq
    return pl.pallas_call(
        _latent_kernel,
        grid=(batch, nq),
        in_specs=[pl.BlockSpec((tq, B_HEADS * LANES), lambda b, i: (b * nq + i, 0)),
                  pl.BlockSpec((seq, B_HEADS * LANES), lambda b, i: (b, 0)),
                  pl.BlockSpec((None, B_WIDTH, seq), lambda b, i: (b, 0, 0)),
                  pl.BlockSpec((tq, B_WIDTH), lambda b, i: (b * nq + i, 1)),
                  pl.BlockSpec((tq, LANES), lambda b, i: (b * nq + i, 0)),
                  pl.BlockSpec((seq, LANES), lambda b, i: (b, 0))],
        out_specs=pl.BlockSpec((tq, B_WIDTH), lambda b, i: (b * nq + i, 0)),
        out_shape=jax.ShapeDtypeStruct((batch * seq, B_WIDTH), jnp.bfloat16),
        compiler_params=pltpu.CompilerParams(
            dimension_semantics=("arbitrary", "arbitrary"), vmem_limit_bytes=VMEM_LIMIT),
        name="latent_attn",
    )(qb, kb, vbt, gate, qn, kn)


def _head_slab(x1, nope, x2):
    z = jnp.zeros((nope.shape[0], LANES - B_NOPE - B_ROPE), nope.dtype)
    return jnp.concatenate([x1, nope[:, :NOPE_A], x2, nope[:, NOPE_A:], z], axis=1)


def _pack_weights(w_in, w_uq, w_ukv):
    bf = jnp.bfloat16
    w_in, w_uq, w_ukv = w_in.astype(bf), w_uq.astype(bf), w_ukv.astype(bf)
    sizes = (A_WIDTH, A_KV_HEADS * HEAD_DIM, A_KV_HEADS * HEAD_DIM, A_WIDTH,
             Q_LORA, KV_LORA, B_ROPE, B_WIDTH)
    offs = np.cumsum((0,) + sizes)
    qa, ka, va, ga, cq, ckv, kr, gb = (w_in[:, offs[i]:offs[i + 1]] for i in range(8))
    zr = jnp.zeros((w_in.shape[0], B_NOPE), w_in.dtype)
    kr_slab = _head_slab(kr[:, :HALF_ROPE], zr, kr[:, HALF_ROPE:])
    w_in_p = jnp.concatenate(
        [qa[:, A_PAIR_COLS], ka, va, ga[:, A_PAIR_COLS], gb, cq, ckv, kr_slab], axis=1)

    pad = lambda w: jnp.pad(w, ((0, 0), (0, 1)))
    w_uq_p = jnp.take(pad(w_uq), UQ_SLAB_COLS, axis=1)
    w_ukv = pad(w_ukv)
    return (w_in_p, w_uq_p, jnp.take(w_ukv, UK_SLAB_COLS, axis=1),
            jnp.take(w_ukv, UV_COLS, axis=1).T)


def _rope_tables(seq):
    pos = np.arange(seq, dtype=np.float32)
    inv_freq = (np.float32(ROPE_BASE) ** (-np.arange(0, B_ROPE, 2, dtype=np.float32) / B_ROPE)
                ).astype(np.float32)
    ang = pos[:, None] * inv_freq[None, :]
    cos, sin = np.cos(ang), np.sin(ang)
    one = np.ones((seq, B_NOPE), np.float32)
    zero = np.zeros((seq, B_NOPE), np.float32)
    pad = np.zeros((seq, LANES - B_NOPE - B_ROPE), np.float32)
    c = np.concatenate([cos, one[:, :NOPE_A], cos, one[:, NOPE_A:], pad], axis=1)
    s = np.concatenate([-sin, zero[:, :NOPE_A], sin, zero[:, NOPE_A:], pad], axis=1)
    qs = np.float32(QB_SCALE)
    return tuple(jnp.asarray(t.astype(np.float32)) for t in (c * qs, s * qs, c, s))


def _window_bias():
    kj = np.arange(3 * BLOCK)[:, None]
    qi = np.arange(BLOCK)[None, :]
    dist = np.abs(qi - kj + BLOCK).astype(np.float32)
    slopes = np.exp2(-8.0 * np.arange(1, A_HEADS + 1, dtype=np.float32) / A_HEADS)
    bias = -slopes[:, None, None] * dist[None] * LOG2E
    bias = np.where((dist <= WINDOW)[None], bias, NEG).astype(np.float32)
    bias = bias.reshape(A_KV_HEADS, A_REP, 3 * BLOCK, BLOCK).transpose(0, 2, 1, 3)
    bias = bias.reshape(A_KV_HEADS, 3 * BLOCK, A_REP * BLOCK)
    first, last = bias.copy(), bias.copy()
    first[:, :BLOCK] = NEG
    last[:, 2 * BLOCK:] = NEG
    return jnp.asarray(np.stack([first, bias, last]))


def kernel(x, norm_mix, w_in, attn_sink, q_a_norm, w_uq, kv_a_norm, w_ukv, w_out, final_norm):
    batch, seq, _ = x.shape
    assert norm_mix.shape[0] == 1, "single-layer block: the final norm is fused into the layer"
    assert seq % TM_PROJ == 0 and seq % TQ_LAT == 0 and seq % CK_LAT == 0
    assert seq % (QB_WIN * BLOCK) == 0
    cq, sq, ck, sk = _rope_tables(seq)
    bias = _window_bias()
    x2 = x.reshape(batch * seq, D_MODEL)
    w_in_p, w_uq_p, w_k_p, w_vt = _pack_weights(w_in[0], w_uq[0], w_ukv[0])
    qa, kva, gate, qb, kb, qn, kn, vbt = _input_proj(
        x2, norm_mix, w_in_p, q_a_norm, w_uq_p, kv_a_norm, w_k_p, w_vt, cq, sq, ck, sk, seq)
    sink_tab = jnp.repeat(attn_sink[0].astype(jnp.float32) * LOG2E, BLOCK).reshape(A_KV_HEADS, A_REP * BLOCK)
    yb = _latent_attn(qb, kb, vbt, gate, qn, kn, batch, seq)
    w_o = w_out[0].astype(jnp.bfloat16)
    out = _window_attn_out(qa, kva, gate, bias, sink_tab, yb, x2,
                           w_o[:A_WIDTH][A_PAIR_COLS], w_o[A_WIDTH:], final_norm[None], batch, seq)
    return out.reshape(batch, seq, D_MODEL)
```

```python
import functools

import jax
import jax.numpy as jnp
import numpy as np
from jax.experimental import pallas as pl
from jax.experimental.pallas import tpu as pltpu

D_MODEL = 1024
HEAD_DIM = 64
BLOCK = 128
A_HEADS = 8
A_KV_HEADS = 2
A_REP = A_HEADS // A_KV_HEADS
A_WIDTH = A_HEADS * HEAD_DIM
WINDOW = 128
B_HEADS = 8
B_NOPE = 64
B_ROPE = 32
B_V = 64
B_WIDTH = B_HEADS * B_V
Q_LORA = 384
KV_LORA = 256
ROPE_BASE = 10000.0
EPS = 1e-6

LANES = 128
HALF_ROPE = B_ROPE // 2
NEG = -1e30
LOG2E = float(np.log2(np.e))
QA_SCALE = HEAD_DIM ** -0.5 * LOG2E
QB_SCALE = (B_NOPE + B_ROPE) ** -0.5 * LOG2E
VMEM_LIMIT = 56 * 1024 * 1024

NOPE_A = 48

A_PAIR_COLS = np.concatenate(
    [np.arange(HEAD_DIM) + (g * A_REP + p) * HEAD_DIM for p in range(A_REP) for g in range(A_KV_HEADS)])


def _slab_cols(x1, nope, x2, zero):
    return np.concatenate([x1, nope[:NOPE_A], x2, nope[NOPE_A:],
                           np.full(LANES - B_NOPE - B_ROPE, zero)])


PER_Q = B_NOPE + B_ROPE
PER_KV = B_NOPE + B_V
UQ_SLAB_COLS = np.concatenate([
    _slab_cols(h * PER_Q + B_NOPE + np.arange(HALF_ROPE), h * PER_Q + np.arange(B_NOPE),
               h * PER_Q + B_NOPE + HALF_ROPE + np.arange(HALF_ROPE), B_HEADS * PER_Q)
    for h in range(B_HEADS)])
UK_SLAB_COLS = np.concatenate([
    _slab_cols(np.full(HALF_ROPE, B_HEADS * PER_KV), h * PER_KV + np.arange(B_NOPE),
               np.full(HALF_ROPE, B_HEADS * PER_KV), B_HEADS * PER_KV)
    for h in range(B_HEADS)])
UV_COLS = np.concatenate([h * PER_KV + B_NOPE + np.arange(B_V) for h in range(B_HEADS)])

C_QA = 0
C_KV = C_QA + A_WIDTH
C_GATE = C_KV + 2 * A_KV_HEADS * HEAD_DIM
C_CQ = C_GATE + A_WIDTH + B_WIDTH
C_CKV = C_CQ + Q_LORA
C_KR = C_CKV + KV_LORA
C_END = C_KR + LANES

TM_PROJ = 512
QB_WIN = 8
GROUP_WIN = 2
TQ_LAT = 1024
CK_LAT = 256
BOUND_SLACK = 1.0 + 2.0 ** -10
L_FLOOR = 2.0 ** -80


def _rms(x, gain):
    return x * jax.lax.rsqrt(jnp.mean(x * x, axis=-1, keepdims=True) + EPS) * gain


def _sq_norm(slab_bf16):
    v = slab_bf16.astype(jnp.float32)
    return jnp.sum(v * v, axis=-1, keepdims=True)


def _rope(slab, cos, sin):
    return slab * cos + pltpu.roll(slab, 64, 1) * sin


def _proj_kernel(x_ref, g_ref, win_ref, qg_ref, wuq_ref, kvg_ref, wk_ref, wvt_ref,
                 cq_ref, sq_ref, ck_ref, sk_ref,
                 qa_ref, kva_ref, gate_ref, qb_ref, kb_ref, qn_ref, kn_ref, vbt_ref):
    h = _rms(x_ref[...], g_ref[...]).astype(jnp.bfloat16)

    def seg(lo, hi):
        return jnp.dot(h, win_ref[:, lo:hi], preferred_element_type=jnp.float32)

    cqn = _rms(seg(C_CQ, C_CKV), qg_ref[...]).astype(jnp.bfloat16)
    ckvn = _rms(seg(C_CKV, C_KR), kvg_ref[...]).astype(jnp.bfloat16)
    kr = _rope(seg(C_KR, C_END), ck_ref[...], sk_ref[...])

    qa_ref[...] = (seg(C_QA, C_KV) * QA_SCALE).astype(jnp.bfloat16)

    q = jnp.dot(cqn, wuq_ref[...], preferred_element_type=jnp.float32)
    cq, sq = cq_ref[...], sq_ref[...]
    lane = jax.lax.broadcasted_iota(jnp.int32, (x_ref.shape[0], LANES), 1)
    qn = jnp.zeros(lane.shape, jnp.float32)
    for hd in range(B_HEADS):
        sl = slice(hd * LANES, (hd + 1) * LANES)
        qh = _rope(q[:, sl], cq, sq).astype(jnp.bfloat16)
        qb_ref[hd] = qh
        qn = jnp.where(lane == hd, _sq_norm(qh), qn)
    qn_ref[...] = qn

    kva_ref[...] = seg(C_KV, C_GATE).astype(jnp.bfloat16)

    kn = jnp.dot(ckvn, wk_ref[...], preferred_element_type=jnp.float32)
    kn2 = jnp.zeros(lane.shape, jnp.float32)
    for hd in range(B_HEADS):
        sl = slice(hd * LANES, (hd + 1) * LANES)
        kh = (kn[:, sl] + kr).astype(jnp.bfloat16)
        kb_ref[hd] = kh
        kn2 = jnp.where(lane == hd, _sq_norm(kh), kn2)
    kn_ref[...] = kn2
    vbt_ref[...] = jax.lax.dot_general(wvt_ref[...], ckvn, (((1,), (1,)), ((), ())),
                                       preferred_element_type=jnp.float32).astype(jnp.bfloat16)

    g = seg(C_GATE, C_CQ)
    gate_ref[...] = (g / (1.0 + jnp.exp(-g))).astype(jnp.bfloat16)


def _input_proj(x2, norm_g, w_in_p, q_g, w_uq_p, kv_g, w_k_p, w_vt, cq, sq, ck, sk, seq):
    t = x2.shape[0]
    tm = TM_PROJ
    n_pos = seq // tm
    full = lambda a: pl.BlockSpec(a.shape, lambda i: (0,) * a.ndim)
    rows = lambda w: pl.BlockSpec((tm, w), lambda i: (i, 0))
    tab = pl.BlockSpec((tm, LANES), lambda i: (i % n_pos, 0))
    bf = jnp.bfloat16
    widths = (A_WIDTH, 2 * A_KV_HEADS * HEAD_DIM, A_WIDTH + B_WIDTH)
    heads = pl.BlockSpec((None, B_HEADS, tm, LANES), lambda i: (i // n_pos, 0, i % n_pos, 0))
    heads_shape = jax.ShapeDtypeStruct((t // seq, B_HEADS, seq, LANES), bf)
    return pl.pallas_call(
        _proj_kernel,
        grid=(t // tm,),
        in_specs=[rows(D_MODEL), full(norm_g), full(w_in_p), full(q_g), full(w_uq_p),
                  full(kv_g), full(w_k_p), full(w_vt), tab, tab, tab, tab],
        out_specs=[rows(w) for w in widths] + [heads, heads, rows(LANES), rows(LANES)]
        + [pl.BlockSpec((None, B_WIDTH, tm), lambda i: (i // n_pos, 0, i % n_pos))],
        out_shape=[jax.ShapeDtypeStruct((t, w), bf) for w in widths]
        + [heads_shape, heads_shape]
        + [jax.ShapeDtypeStruct((t, LANES), jnp.float32)] * 2
        + [jax.ShapeDtypeStruct((t // seq, B_WIDTH, seq), bf)],
        compiler_params=pltpu.CompilerParams(
            dimension_semantics=("arbitrary",), vmem_limit_bytes=VMEM_LIMIT),
        name="input_proj",
    )(x2, norm_g, w_in_p, q_g, w_uq_p, kv_g, w_k_p, w_vt, cq, sq, ck, sk)


class _WindowStages:
    def __init__(self, refs, step, n_blocks):
        (self.q_ref, self.kva_ref, self.gate_ref, self.bias_ref, self.sink_ref, self.yb_ref,
         self.x_ref, self.wa_ref, self.wb_ref, self.g_ref, self.o_ref, self.vt_ref,
         self.ya_ref) = refs
        self.step = step
        self.n_blocks = n_blocks
        self.first = jax.lax.broadcasted_iota(jnp.int32, (BLOCK, LANES), 1) < HEAD_DIM

    def _band(self, blk):
        n = self.step * QB_WIN + blk
        return (jnp.maximum(n - 1, 0), n, jnp.minimum(n + 1, self.n_blocks - 1))

    def _bias_idx(self, blk):
        if blk == 0:
            return jnp.where(self.step == 0, 0, 1)
        if blk == QB_WIN - 1:
            return jnp.where(self.step == self.n_blocks // QB_WIN - 1, 2, 1)
        return 1

    def scores(self, blk):
        rows = pl.ds(blk * BLOCK, BLOCK)
        kk = jnp.concatenate(
            [self.kva_ref[pl.ds(pl.multiple_of(b * BLOCK, BLOCK), BLOCK), :LANES]
             for b in self._band(blk)], axis=0)
        qp = [self.q_ref[rows, p * LANES:(p + 1) * LANES] for p in range(A_REP)]
        zero = jnp.zeros((), jnp.bfloat16)
        out = []
        for g in range(A_KV_HEADS):
            keep = self.first if g == 0 else jnp.logical_not(self.first)
            q4 = jnp.concatenate([jnp.where(keep, x, zero) for x in qp], axis=0)
            st = jax.lax.dot_general(kk, q4, (((1,), (1,)), ((), ())),
                                     preferred_element_type=jnp.float32)
            out.append(st + self.bias_ref[self._bias_idx(blk), g])
        return out

    def softmax(self, sts):
        out = []
        for g, st in enumerate(sts):
            sink = self.sink_ref[g:g + 1, :]
            m = jnp.maximum(jnp.max(st, axis=0, keepdims=True), sink)
            p = jnp.exp2(st - m)
            inv = 1.0 / (jnp.sum(p, axis=0, keepdims=True) + jnp.exp2(sink - m))
            out.append((p.astype(jnp.bfloat16), inv))
        return out

    def values(self, blk, probs):
        rows = pl.ds(blk * BLOCK, BLOCK)
        vts = [self.vt_ref[b] for b in self._band(blk)]
        outs = []
        for g, (pb, inv) in enumerate(probs):
            vt = jnp.concatenate([x[g * HEAD_DIM:(g + 1) * HEAD_DIM, :] for x in vts], axis=1)
            outs.append(jnp.dot(vt, pb, preferred_element_type=jnp.float32) * inv)
        for pr in range(A_REP):
            sl = slice(pr * LANES, (pr + 1) * LANES)
            o = jnp.concatenate([outs[0][:, sl], outs[1][:, sl]], axis=0).T
            self.ya_ref[rows, sl] = (o * self.gate_ref[rows, sl].astype(jnp.float32)
                                     ).astype(jnp.bfloat16)

    def out_proj(self, rows):
        d = jnp.dot(self.ya_ref[rows, :], self.wa_ref[...], preferred_element_type=jnp.float32)
        d = d + jnp.dot(self.yb_ref[rows, :], self.wb_ref[...], preferred_element_type=jnp.float32)
        return self.x_ref[rows, :] + d

    def norm_store(self, rows, y):
        self.o_ref[rows, :] = _rms(y, self.g_ref[...])


def _window_out_kernel(*refs, n_blocks):
    kva_ref, vt_ref = refs[1], refs[11]
    step = pl.program_id(1)

    @pl.when(step == 0)
    def _():
        for b in range(n_blocks):
            v = kva_ref[b * BLOCK:(b + 1) * BLOCK, LANES:].astype(jnp.float32)
            vt_ref[b] = v.T.astype(jnp.bfloat16)

    st = _WindowStages(refs, step, n_blocks)
    groups = [list(range(g, g + GROUP_WIN)) for g in range(0, QB_WIN, GROUP_WIN)]
    rows = [pl.ds(g[0] * BLOCK, len(g) * BLOCK) for g in groups]
    state = [None] * len(groups)
    stages = [
        lambda i: [st.scores(b) for b in groups[i]],
        lambda i: [st.softmax(s) for s in state[i]],
        lambda i: [st.values(b, p) for b, p in zip(groups[i], state[i])],
        lambda i: st.out_proj(rows[i]),
        lambda i: st.norm_store(rows[i], state[i]),
    ]
    for tick in range(len(groups) + len(stages) - 1):
        for s in reversed(range(len(stages))):
            i = tick - s
            if 0 <= i < len(groups):
                state[i] = stages[s](i)


def _window_attn_out(qa, kva, gate, bias, sink_tab, yb, x2, w_a, w_b, final_g, batch, seq):
    nb = seq // BLOCK
    ns = nb // QB_WIN
    tq = QB_WIN * BLOCK
    rows = lambda w: pl.BlockSpec((tq, w), lambda b, n: (b * ns + n, 0))
    full = lambda a: pl.BlockSpec(a.shape, lambda b, n: (0,) * a.ndim)
    return pl.pallas_call(
        functools.partial(_window_out_kernel, n_blocks=nb),
        grid=(batch, ns),
        in_specs=[rows(A_WIDTH),
                  pl.BlockSpec((seq, 2 * A_KV_HEADS * HEAD_DIM), lambda b, n: (b, 0)),
                  rows(A_WIDTH), full(bias), full(sink_tab), rows(B_WIDTH), rows(D_MODEL),
                  full(w_a), full(w_b), full(final_g)],
        out_specs=rows(D_MODEL),
        out_shape=jax.ShapeDtypeStruct((batch * seq, D_MODEL), jnp.float32),
        scratch_shapes=[pltpu.VMEM((nb, LANES, BLOCK), jnp.bfloat16),
                        pltpu.VMEM((tq, A_WIDTH), jnp.bfloat16)],
        compiler_params=pltpu.CompilerParams(
            dimension_semantics=("arbitrary", "arbitrary"), vmem_limit_bytes=VMEM_LIMIT),
        name="window_attn_out",
    )(qa, kva, gate, bias, sink_tab, yb, x2, w_a, w_b, final_g)


def _latent_kernel(q_ref, k_ref, vt_ref, gate_ref, qn_ref, kn_ref, o_ref):
    n_chunks = k_ref.shape[1] // CK_LAT

    def scores(hd):
        return jax.lax.dot_general(k_ref[hd], q_ref[hd], (((1,), (1,)), ((), ())),
                                   preferred_element_type=jnp.float32)

    def weighted_values(hd, st, shift):
        l = acc = None
        for c in range(n_chunks):
            keys = slice(c * CK_LAT, (c + 1) * CK_LAT)
            p = jnp.exp2(st[keys] - shift)
            lc = jnp.sum(p, axis=0, keepdims=True)
            ac = jnp.dot(vt_ref[hd * B_V:(hd + 1) * B_V, keys], p.astype(jnp.bfloat16),
                         preferred_element_type=jnp.float32)
            l = lc if c == 0 else l + lc
            acc = ac if c == 0 else acc + ac
        return acc, l

    def store_pair(hd, outs):
        o = jnp.concatenate(outs, axis=0).T
        sl = slice((hd // 2) * LANES, (hd // 2 + 1) * LANES)
        o_ref[:, sl] = (o * gate_ref[:, sl].astype(jnp.float32)).astype(jnp.bfloat16)

    k_norm2 = jnp.max(kn_ref[...], axis=0, keepdims=True)
    q_norm2 = qn_ref[...].T
    outs, l_min = [], None
    for hd in range(B_HEADS):
        shift = jnp.sqrt(q_norm2[hd:hd + 1, :] * k_norm2[:, hd:hd + 1]) * BOUND_SLACK
        acc, l = weighted_values(hd, scores(hd), shift)
        l_min = l if hd == 0 else jnp.minimum(l_min, l)
        outs.append(acc * (1.0 / l))
        if hd % 2 == 1:
            store_pair(hd, outs)
            outs = []

    @pl.when(jnp.logical_not(jnp.min(l_min) >= L_FLOOR))
    def _():
        outs = []
        nxt = scores(0)
        for hd in range(B_HEADS):
            st = nxt
            if hd + 1 < B_HEADS:
                nxt = scores(hd + 1)
            acc, l = weighted_values(hd, st, jnp.max(st, axis=0, keepdims=True))
            outs.append(acc * (1.0 / l))
            if hd % 2 == 1:
                store_pair(hd, outs)
                outs = []


def _latent_attn(qb, kb, vbt, gate, qn, kn, batch, seq):
    tq = TQ_LAT
    nq = seq // tq
    return pl.pallas_call(
        _latent_kernel,
        grid=(batch, nq),
        in_specs=[pl.BlockSpec((None, B_HEADS, tq, LANES), lambda b, i: (b, 0, i, 0)),
                  pl.BlockSpec((None, B_HEADS, seq, LANES), lambda b, i: (b, 0, 0, 0)),
                  pl.BlockSpec((None, B_WIDTH, seq), lambda b, i: (b, 0, 0)),
                  pl.BlockSpec((tq, B_WIDTH), lambda b, i: (b * nq + i, 1)),
                  pl.BlockSpec((tq, LANES), lambda b, i: (b * nq + i, 0)),
                  pl.BlockSpec((seq, LANES), lambda b, i: (b, 0))],
        out_specs=pl.BlockSpec((tq, B_WIDTH), lambda b, i: (b * nq + i, 0)),
        out_shape=jax.ShapeDtypeStruct((batch * seq, B_WIDTH), jnp.bfloat16),
        compiler_params=pltpu.CompilerParams(
            dimension_semantics=("arbitrary", "arbitrary"), vmem_limit_bytes=VMEM_LIMIT),
        name="latent_attn",
    )(qb, kb, vbt, gate, qn, kn)


def _head_slab(x1, nope, x2):
    z = jnp.zeros((nope.shape[0], LANES - B_NOPE - B_ROPE), nope.dtype)
    return jnp.concatenate([x1, nope[:, :NOPE_A], x2, nope[:, NOPE_A:], z], axis=1)


def _pack_weights(w_in, w_uq, w_ukv):
    bf = jnp.bfloat16
    w_in, w_uq, w_ukv = w_in.astype(bf), w_uq.astype(bf), w_ukv.astype(bf)
    sizes = (A_WIDTH, A_KV_HEADS * HEAD_DIM, A_KV_HEADS * HEAD_DIM, A_WIDTH,
             Q_LORA, KV_LORA, B_ROPE, B_WIDTH)
    offs = np.cumsum((0,) + sizes)
    qa, ka, va, ga, cq, ckv, kr, gb = (w_in[:, offs[i]:offs[i + 1]] for i in range(8))
    zr = jnp.zeros((w_in.shape[0], B_NOPE), w_in.dtype)
    kr_slab = _head_slab(kr[:, :HALF_ROPE], zr, kr[:, HALF_ROPE:])
    w_in_p = jnp.concatenate(
        [qa[:, A_PAIR_COLS], ka, va, ga[:, A_PAIR_COLS], gb, cq, ckv, kr_slab], axis=1)

    pad = lambda w: jnp.pad(w, ((0, 0), (0, 1)))
    w_uq_p = jnp.take(pad(w_uq), UQ_SLAB_COLS, axis=1)
    w_ukv = pad(w_ukv)
    return (w_in_p, w_uq_p, jnp.take(w_ukv, UK_SLAB_COLS, axis=1),
            jnp.take(w_ukv, UV_COLS, axis=1).T)


def _rope_tables(seq):
    pos = np.arange(seq, dtype=np.float32)
    inv_freq = (np.float32(ROPE_BASE) ** (-np.arange(0, B_ROPE, 2, dtype=np.float32) / B_ROPE)
                ).astype(np.float32)
    ang = pos[:, None] * inv_freq[None, :]
    cos, sin = np.cos(ang), np.sin(ang)
    one = np.ones((seq, B_NOPE), np.float32)
    zero = np.zeros((seq, B_NOPE), np.float32)
    pad = np.zeros((seq, LANES - B_NOPE - B_ROPE), np.float32)
    c = np.concatenate([cos, one[:, :NOPE_A], cos, one[:, NOPE_A:], pad], axis=1)
    s = np.concatenate([-sin, zero[:, :NOPE_A], sin, zero[:, NOPE_A:], pad], axis=1)
    qs = np.float32(QB_SCALE)
    return tuple(jnp.asarray(t.astype(np.float32)) for t in (c * qs, s * qs, c, s))


def _window_bias():
    kj = np.arange(3 * BLOCK)[:, None]
    qi = np.arange(BLOCK)[None, :]
    dist = np.abs(qi - kj + BLOCK).astype(np.float32)
    slopes = np.exp2(-8.0 * np.arange(1, A_HEADS + 1, dtype=np.float32) / A_HEADS)
    bias = -slopes[:, None, None] * dist[None] * LOG2E
    bias = np.where((dist <= WINDOW)[None], bias, NEG).astype(np.float32)
    bias = bias.reshape(A_KV_HEADS, A_REP, 3 * BLOCK, BLOCK).transpose(0, 2, 1, 3)
    bias = bias.reshape(A_KV_HEADS, 3 * BLOCK, A_REP * BLOCK)
    first, last = bias.copy(), bias.copy()
    first[:, :BLOCK] = NEG
    last[:, 2 * BLOCK:] = NEG
    return jnp.asarray(np.stack([first, bias, last]))


def kernel(x, norm_mix, w_in, attn_sink, q_a_norm, w_uq, kv_a_norm, w_ukv, w_out, final_norm):
    batch, seq, _ = x.shape
    assert norm_mix.shape[0] == 1, "single-layer block: the final norm is fused into the layer"
    assert seq % TM_PROJ == 0 and seq % TQ_LAT == 0 and seq % CK_LAT == 0
    assert seq % (QB_WIN * BLOCK) == 0
    cq, sq, ck, sk = _rope_tables(seq)
    bias = _window_bias()
    x2 = x.reshape(batch * seq, D_MODEL)
    w_in_p, w_uq_p, w_k_p, w_vt = _pack_weights(w_in[0], w_uq[0], w_ukv[0])
    qa, kva, gate, qb, kb, qn, kn, vbt = _input_proj(
        x2, norm_mix, w_in_p, q_a_norm, w_uq_p, kv_a_norm, w_k_p, w_vt, cq, sq, ck, sk, seq)
    sink_tab = jnp.repeat(attn_sink[0].astype(jnp.float32) * LOG2E, BLOCK).reshape(A_KV_HEADS, A_REP * BLOCK)
    yb = _latent_attn(qb, kb, vbt, gate, qn, kn, batch, seq)
    w_o = w_out[0].astype(jnp.bfloat16)
    out = _window_attn_out(qa, kva, gate, bias, sink_tab, yb, x2,
                           w_o[:A_WIDTH][A_PAIR_COLS], w_o[A_WIDTH:], final_norm[None], batch, seq)
    return out.reshape(batch, seq, D_MODEL)
```

```python
import functools

import jax
import jax.numpy as jnp
import numpy as np
from jax.experimental import pallas as pl
from jax.experimental.pallas import tpu as pltpu

D_MODEL = 1024
HEAD_DIM = 64
BLOCK = 128
A_HEADS = 8
A_KV_HEADS = 2
A_REP = A_HEADS // A_KV_HEADS
A_WIDTH = A_HEADS * HEAD_DIM
WINDOW = 128
B_HEADS = 8
B_NOPE = 64
B_ROPE = 32
B_V = 64
B_WIDTH = B_HEADS * B_V
Q_LORA = 384
KV_LORA = 256
ROPE_BASE = 10000.0
EPS = 1e-6

LANES = 128
HALF_ROPE = B_ROPE // 2
NEG = -1e30
LOG2E = float(np.log2(np.e))
QA_SCALE = HEAD_DIM ** -0.5 * LOG2E
QB_SCALE = (B_NOPE + B_ROPE) ** -0.5 * LOG2E
VMEM_LIMIT = 56 * 1024 * 1024

NOPE_A = 48

A_PAIR_COLS = np.concatenate(
    [np.arange(HEAD_DIM) + (g * A_REP + p) * HEAD_DIM for p in range(A_REP) for g in range(A_KV_HEADS)])


def _slab_cols(x1, nope, x2, zero):
    return np.concatenate([x1, nope[:NOPE_A], x2, nope[NOPE_A:],
                           np.full(LANES - B_NOPE - B_ROPE, zero)])


PER_Q = B_NOPE + B_ROPE
PER_KV = B_NOPE + B_V
UQ_SLAB_COLS = np.concatenate([
    _slab_cols(h * PER_Q + B_NOPE + np.arange(HALF_ROPE), h * PER_Q + np.arange(B_NOPE),
               h * PER_Q + B_NOPE + HALF_ROPE + np.arange(HALF_ROPE), B_HEADS * PER_Q)
    for h in range(B_HEADS)])
UK_SLAB_COLS = np.concatenate([
    _slab_cols(np.full(HALF_ROPE, B_HEADS * PER_KV), h * PER_KV + np.arange(B_NOPE),
               np.full(HALF_ROPE, B_HEADS * PER_KV), B_HEADS * PER_KV)
    for h in range(B_HEADS)])
UV_COLS = np.concatenate([h * PER_KV + B_NOPE + np.arange(B_V) for h in range(B_HEADS)])

C_QA = 0
C_KV = C_QA + A_WIDTH
C_GATE = C_KV + 2 * A_KV_HEADS * HEAD_DIM
C_CQ = C_GATE + A_WIDTH + B_WIDTH
C_CKV = C_CQ + Q_LORA
C_KR = C_CKV + KV_LORA
C_END = C_KR + LANES

TM_PROJ = 1024
ROWS_PROJ = 256
QB_WIN = 8
GROUP_WIN = 2
TQ_LAT = 1024
CK_LAT = 256
BOUND_SLACK = 1.0 + 2.0 ** -10
L_FLOOR = 2.0 ** -80


def _rms(x, gain):
    return x * jax.lax.rsqrt(jnp.mean(x * x, axis=-1, keepdims=True) + EPS) * gain


def _emit_pipeline(n_groups, stages):
    state = [None] * n_groups
    for tick in range(n_groups + len(stages) - 1):
        for s in reversed(range(len(stages))):
            i = tick - s
            if 0 <= i < n_groups:
                state[i] = stages[s](i, state[i])


def _sq_norm(slab_bf16):
    v = slab_bf16.astype(jnp.float32)
    return jnp.sum(v * v, axis=-1, keepdims=True)


def _rope(slab, cos, sin):
    return slab * cos + pltpu.roll(slab, 64, 1) * sin


def _proj_kernel(x_ref, g_ref, win_ref, qg_ref, wuq_ref, kvg_ref, wk_ref, wvt_ref,
                 cq_ref, sq_ref, ck_ref, sk_ref,
                 qa_ref, kva_ref, gate_ref, qb_ref, kb_ref, qn_ref, kn_ref, vbt_ref):
    bf = jnp.bfloat16
    lane = jax.lax.broadcasted_iota(jnp.int32, (ROWS_PROJ, LANES), 1)
    rows = lambda i: pl.ds(i * ROWS_PROJ, ROWS_PROJ)

    def seg(h, lo, hi):
        return jnp.dot(h, win_ref[:, lo:hi], preferred_element_type=jnp.float32)

    def head_slabs(i, slab_of, out_ref, norm_ref):
        norms = jnp.zeros(lane.shape, jnp.float32)
        for hd in range(B_HEADS):
            slab = slab_of(slice(hd * LANES, (hd + 1) * LANES)).astype(bf)
            out_ref[hd, rows(i), :] = slab
            norms = jnp.where(lane == hd, _sq_norm(slab), norms)
        norm_ref[rows(i), :] = norms

    def s_norm(i, _):
        return _rms(x_ref[rows(i), :], g_ref[...]).astype(bf)

    def s_latent(i, h):
        cqn = _rms(seg(h, C_CQ, C_CKV), qg_ref[...]).astype(bf)
        ckvn = _rms(seg(h, C_CKV, C_KR), kvg_ref[...]).astype(bf)
        kr = _rope(seg(h, C_KR, C_END), ck_ref[rows(i), :], sk_ref[rows(i), :])
        return h, cqn, ckvn, kr

    def s_qa(i, st):
        qa_ref[rows(i), :] = (seg(st[0], C_QA, C_KV) * QA_SCALE).astype(bf)
        return st

    def s_q(i, st):
        q = jnp.dot(st[1], wuq_ref[...], preferred_element_type=jnp.float32)
        cq, sq = cq_ref[rows(i), :], sq_ref[rows(i), :]
        head_slabs(i, lambda sl: _rope(q[:, sl], cq, sq), qb_ref, qn_ref)
        return st

    def s_kva(i, st):
        kva_ref[rows(i), :] = seg(st[0], C_KV, C_GATE).astype(bf)
        return st

    def s_k(i, st):
        kn = jnp.dot(st[2], wk_ref[...], preferred_element_type=jnp.float32)
        head_slabs(i, lambda sl: kn[:, sl] + st[3], kb_ref, kn_ref)
        return st

    def s_v(i, st):
        vbt_ref[:, rows(i)] = jax.lax.dot_general(
            wvt_ref[...], st[2], (((1,), (1,)), ((), ())),
            preferred_element_type=jnp.float32).astype(bf)
        return st

    def s_gate(i, st):
        g = seg(st[0], C_GATE, C_CQ)
        gate_ref[rows(i), :] = (g / (1.0 + jnp.exp(-g))).astype(bf)
        return st

    _emit_pipeline(x_ref.shape[0] // ROWS_PROJ,
                   [s_norm, s_latent, s_qa, s_q, s_kva, s_k, s_v, s_gate])


def _input_proj(x2, norm_g, w_in_p, q_g, w_uq_p, kv_g, w_k_p, w_vt, cq, sq, ck, sk, seq):
    t = x2.shape[0]
    tm = TM_PROJ
    n_pos = seq // tm
    full = lambda a: pl.BlockSpec(a.shape, lambda i: (0,) * a.ndim)
    rows = lambda w: pl.BlockSpec((tm, w), lambda i: (i, 0))
    tab = pl.BlockSpec((tm, LANES), lambda i: (i % n_pos, 0))
    bf = jnp.bfloat16
    widths = (A_WIDTH, 2 * A_KV_HEADS * HEAD_DIM, A_WIDTH + B_WIDTH)
    heads = pl.BlockSpec((None, B_HEADS, tm, LANES), lambda i: (i // n_pos, 0, i % n_pos, 0))
    heads_shape = jax.ShapeDtypeStruct((t // seq, B_HEADS, seq, LANES), bf)
    return pl.pallas_call(
        _proj_kernel,
        grid=(t // tm,),
        in_specs=[rows(D_MODEL), full(norm_g), full(w_in_p), full(q_g), full(w_uq_p),
                  full(kv_g), full(w_k_p), full(w_vt), tab, tab, tab, tab],
        out_specs=[rows(w) for w in widths] + [heads, heads, rows(LANES), rows(LANES)]
        + [pl.BlockSpec((None, B_WIDTH, tm), lambda i: (i // n_pos, 0, i % n_pos))],
        out_shape=[jax.ShapeDtypeStruct((t, w), bf) for w in widths]
        + [heads_shape, heads_shape]
        + [jax.ShapeDtypeStruct((t, LANES), jnp.float32)] * 2
        + [jax.ShapeDtypeStruct((t // seq, B_WIDTH, seq), bf)],
        compiler_params=pltpu.CompilerParams(
            dimension_semantics=("arbitrary",), vmem_limit_bytes=VMEM_LIMIT),
        name="input_proj",
    )(x2, norm_g, w_in_p, q_g, w_uq_p, kv_g, w_k_p, w_vt, cq, sq, ck, sk)


class _WindowStages:
    def __init__(self, refs, step, n_blocks):
        (self.q_ref, self.kva_ref, self.gate_ref, self.bias_ref, self.sink_ref, self.yb_ref,
         self.x_ref, self.wa_ref, self.wb_ref, self.g_ref, self.o_ref, self.vt_ref,
         self.ya_ref) = refs
        self.step = step
        self.n_blocks = n_blocks
        self.first = jax.lax.broadcasted_iota(jnp.int32, (BLOCK, LANES), 1) < HEAD_DIM

    def _band(self, blk):
        n = self.step * QB_WIN + blk
        return (jnp.maximum(n - 1, 0), n, jnp.minimum(n + 1, self.n_blocks - 1))

    def _bias_idx(self, blk):
        if blk == 0:
            return jnp.where(self.step == 0, 0, 1)
        if blk == QB_WIN - 1:
            return jnp.where(self.step == self.n_blocks // QB_WIN - 1, 2, 1)
        return 1

    def scores(self, blk):
        rows = pl.ds(blk * BLOCK, BLOCK)
        kk = jnp.concatenate(
            [self.kva_ref[pl.ds(pl.multiple_of(b * BLOCK, BLOCK), BLOCK), :LANES]
             for b in self._band(blk)], axis=0)
        qp = [self.q_ref[rows, p * LANES:(p + 1) * LANES] for p in range(A_REP)]
        zero = jnp.zeros((), jnp.bfloat16)
        out = []
        for g in range(A_KV_HEADS):
            keep = self.first if g == 0 else jnp.logical_not(self.first)
            q4 = jnp.concatenate([jnp.where(keep, x, zero) for x in qp], axis=0)
            st = jax.lax.dot_general(kk, q4, (((1,), (1,)), ((), ())),
                                     preferred_element_type=jnp.float32)
            out.append(st + self.bias_ref[self._bias_idx(blk), g])
        return out

    def softmax(self, sts):
        out = []
        for g, st in enumerate(sts):
            sink = self.sink_ref[g:g + 1, :]
            m = jnp.maximum(jnp.max(st, axis=0, keepdims=True), sink)
            p = jnp.exp2(st - m)
            inv = 1.0 / (jnp.sum(p, axis=0, keepdims=True) + jnp.exp2(sink - m))
            out.append((p.astype(jnp.bfloat16), inv))
        return out

    def values(self, blk, probs):
        rows = pl.ds(blk * BLOCK, BLOCK)
        vts = [self.vt_ref[b] for b in self._band(blk)]
        outs = []
        for g, (pb, inv) in enumerate(probs):
            vt = jnp.concatenate([x[g * HEAD_DIM:(g + 1) * HEAD_DIM, :] for x in vts], axis=1)
            outs.append(jnp.dot(vt, pb, preferred_element_type=jnp.float32) * inv)
        for pr in range(A_REP):
            sl = slice(pr * LANES, (pr + 1) * LANES)
            o = jnp.concatenate([outs[0][:, sl], outs[1][:, sl]], axis=0).T
            self.ya_ref[rows, sl] = (o * self.gate_ref[rows, sl].astype(jnp.float32)
                                     ).astype(jnp.bfloat16)

    def out_proj(self, rows):
        d = jnp.dot(self.ya_ref[rows, :], self.wa_ref[...], preferred_element_type=jnp.float32)
        d = d + jnp.dot(self.yb_ref[rows, :], self.wb_ref[...], preferred_element_type=jnp.float32)
        return self.x_ref[rows, :] + d

    def norm_store(self, rows, y):
        self.o_ref[rows, :] = _rms(y, self.g_ref[...])


def _window_out_kernel(*refs, n_blocks):
    kva_ref, vt_ref = refs[1], refs[11]
    step = pl.program_id(1)

    @pl.when(step == 0)
    def _():
        for b in range(n_blocks):
            v = kva_ref[b * BLOCK:(b + 1) * BLOCK, LANES:].astype(jnp.float32)
            vt_ref[b] = v.T.astype(jnp.bfloat16)

    st = _WindowStages(refs, step, n_blocks)
    groups = [list(range(g, g + GROUP_WIN)) for g in range(0, QB_WIN, GROUP_WIN)]
    rows = [pl.ds(g[0] * BLOCK, len(g) * BLOCK) for g in groups]
    _emit_pipeline(len(groups), [
        lambda i, _: [st.scores(b) for b in groups[i]],
        lambda i, scores: [st.softmax(s) for s in scores],
        lambda i, probs: [st.values(b, p) for b, p in zip(groups[i], probs)],
        lambda i, _: st.out_proj(rows[i]),
        lambda i, y: st.norm_store(rows[i], y),
    ])


def _window_attn_out(qa, kva, gate, bias, sink_tab, yb, x2, w_a, w_b, final_g, batch, seq):
    nb = seq // BLOCK
    ns = nb // QB_WIN
    tq = QB_WIN * BLOCK
    rows = lambda w: pl.BlockSpec((tq, w), lambda b, n: (b * ns + n, 0))
    full = lambda a: pl.BlockSpec(a.shape, lambda b, n: (0,) * a.ndim)
    return pl.pallas_call(
        functools.partial(_window_out_kernel, n_blocks=nb),
        grid=(batch, ns),
        in_specs=[rows(A_WIDTH),
                  pl.BlockSpec((seq, 2 * A_KV_HEADS * HEAD_DIM), lambda b, n: (b, 0)),
                  rows(A_WIDTH), full(bias), full(sink_tab), rows(B_WIDTH), rows(D_MODEL),
                  full(w_a), full(w_b), full(final_g)],
        out_specs=rows(D_MODEL),
        out_shape=jax.ShapeDtypeStruct((batch * seq, D_MODEL), jnp.float32),
        scratch_shapes=[pltpu.VMEM((nb, LANES, BLOCK), jnp.bfloat16),
                        pltpu.VMEM((tq, A_WIDTH), jnp.bfloat16)],
        compiler_params=pltpu.CompilerParams(
            dimension_semantics=("arbitrary", "arbitrary"), vmem_limit_bytes=VMEM_LIMIT),
        name="window_attn_out",
    )(qa, kva, gate, bias, sink_tab, yb, x2, w_a, w_b, final_g)


def _latent_kernel(q_ref, k_ref, vt_ref, gate_ref, qn_ref, kn_ref, o_ref):
    n_chunks = k_ref.shape[1] // CK_LAT

    def scores(hd):
        return jax.lax.dot_general(k_ref[hd], q_ref[hd], (((1,), (1,)), ((), ())),
                                   preferred_element_type=jnp.float32)

    def weighted_values(hd, st, shift):
        l = acc = None
        for c in range(n_chunks):
            keys = slice(c * CK_LAT, (c + 1) * CK_LAT)
            p = jnp.exp2(st[keys] - shift)
            lc = jnp.sum(p, axis=0, keepdims=True)
            ac = jnp.dot(vt_ref[hd * B_V:(hd + 1) * B_V, keys], p.astype(jnp.bfloat16),
                         preferred_element_type=jnp.float32)
            l = lc if c == 0 else l + lc
            acc = ac if c == 0 else acc + ac
        return acc, l

    def store_pair(hd, outs):
        o = jnp.concatenate(outs, axis=0).T
        sl = slice((hd // 2) * LANES, (hd // 2 + 1) * LANES)
        o_ref[:, sl] = (o * gate_ref[:, sl].astype(jnp.float32)).astype(jnp.bfloat16)

    k_norm2 = jnp.max(kn_ref[...], axis=0, keepdims=True)
    q_norm2 = qn_ref[...].T
    outs, l_min = [], None
    for hd in range(B_HEADS):
        shift = jnp.sqrt(q_norm2[hd:hd + 1, :] * k_norm2[:, hd:hd + 1]) * BOUND_SLACK
        acc, l = weighted_values(hd, scores(hd), shift)
        l_min = l if hd == 0 else jnp.minimum(l_min, l)
        outs.append(acc * (1.0 / l))
        if hd % 2 == 1:
            store_pair(hd, outs)
            outs = []

    @pl.when(jnp.logical_not(jnp.min(l_min) >= L_FLOOR))
    def _():
        outs = []
        nxt = scores(0)
        for hd in range(B_HEADS):
            st = nxt
            if hd + 1 < B_HEADS:
                nxt = scores(hd + 1)
            acc, l = weighted_values(hd, st, jnp.max(st, axis=0, keepdims=True))
            outs.append(acc * (1.0 / l))
            if hd % 2 == 1:
                store_pair(hd, outs)
                outs = []


def _latent_attn(qb, kb, vbt, gate, qn, kn, batch, seq):
    tq = TQ_LAT
    nq = seq // tq
    return pl.pallas_call(
        _latent_kernel,
        grid=(batch, nq),
        in_specs=[pl.BlockSpec((None, B_HEADS, tq, LANES), lambda b, i: (b, 0, i, 0)),
                  pl.BlockSpec((None, B_HEADS, seq, LANES), lambda b, i: (b, 0, 0, 0)),
                  pl.BlockSpec((None, B_WIDTH, seq), lambda b, i: (b, 0, 0)),
                  pl.BlockSpec((tq, B_WIDTH), lambda b, i: (b * nq + i, 1)),
                  pl.BlockSpec((tq, LANES), lambda b, i: (b * nq + i, 0)),
                  pl.BlockSpec((seq, LANES), lambda b, i: (b, 0))],
        out_specs=pl.BlockSpec((tq, B_WIDTH), lambda b, i: (b * nq + i, 0)),
        out_shape=jax.ShapeDtypeStruct((batch * seq, B_WIDTH), jnp.bfloat16),
        compiler_params=pltpu.CompilerParams(
            dimension_semantics=("arbitrary", "arbitrary"), vmem_limit_bytes=VMEM_LIMIT),
        name="latent_attn",
    )(qb, kb, vbt, gate, qn, kn)


def _head_slab(x1, nope, x2):
    z = jnp.zeros((nope.shape[0], LANES - B_NOPE - B_ROPE), nope.dtype)
    return jnp.concatenate([x1, nope[:, :NOPE_A], x2, nope[:, NOPE_A:], z], axis=1)


def _pack_weights(w_in, w_uq, w_ukv):
    bf = jnp.bfloat16
    w_in, w_uq, w_ukv = w_in.astype(bf), w_uq.astype(bf), w_ukv.astype(bf)
    sizes = (A_WIDTH, A_KV_HEADS * HEAD_DIM, A_KV_HEADS * HEAD_DIM, A_WIDTH,
             Q_LORA, KV_LORA, B_ROPE, B_WIDTH)
    offs = np.cumsum((0,) + sizes)
    qa, ka, va, ga, cq, ckv, kr, gb = (w_in[:, offs[i]:offs[i + 1]] for i in range(8))
    zr = jnp.zeros((w_in.shape[0], B_NOPE), w_in.dtype)
    kr_slab = _head_slab(kr[:, :HALF_ROPE], zr, kr[:, HALF_ROPE:])
    w_in_p = jnp.concatenate(
        [qa[:, A_PAIR_COLS], ka, va, ga[:, A_PAIR_COLS], gb, cq, ckv, kr_slab], axis=1)

    pad = lambda w: jnp.pad(w, ((0, 0), (0, 1)))
    w_uq_p = jnp.take(pad(w_uq), UQ_SLAB_COLS, axis=1)
    w_ukv = pad(w_ukv)
    return (w_in_p, w_uq_p, jnp.take(w_ukv, UK_SLAB_COLS, axis=1),
            jnp.take(w_ukv, UV_COLS, axis=1).T)


def _rope_tables(seq):
    pos = np.arange(seq, dtype=np.float32)
    inv_freq = (np.float32(ROPE_BASE) ** (-np.arange(0, B_ROPE, 2, dtype=np.float32) / B_ROPE)
                ).astype(np.float32)
    ang = pos[:, None] * inv_freq[None, :]
    cos, sin = np.cos(ang), np.sin(ang)
    one = np.ones((seq, B_NOPE), np.float32)
    zero = np.zeros((seq, B_NOPE), np.float32)
    pad = np.zeros((seq, LANES - B_NOPE - B_ROPE), np.float32)
    c = np.concatenate([cos, one[:, :NOPE_A], cos, one[:, NOPE_A:], pad], axis=1)
    s = np.concatenate([-sin, zero[:, :NOPE_A], sin, zero[:, NOPE_A:], pad], axis=1)
    qs = np.float32(QB_SCALE)
    return tuple(jnp.asarray(t.astype(np.float32)) for t in (c * qs, s * qs, c, s))


def _window_bias():
    kj = np.arange(3 * BLOCK)[:, None]
    qi = np.arange(BLOCK)[None, :]
    dist = np.abs(qi - kj + BLOCK).astype(np.float32)
    slopes = np.exp2(-8.0 * np.arange(1, A_HEADS + 1, dtype=np.float32) / A_HEADS)
    bias = -slopes[:, None, None] * dist[None] * LOG2E
    bias = np.where((dist <= WINDOW)[None], bias, NEG).astype(np.float32)
    bias = bias.reshape(A_KV_HEADS, A_REP, 3 * BLOCK, BLOCK).transpose(0, 2, 1, 3)
    bias = bias.reshape(A_KV_HEADS, 3 * BLOCK, A_REP * BLOCK)
    first, last = bias.copy(), bias.copy()
    first[:, :BLOCK] = NEG
    last[:, 2 * BLOCK:] = NEG
    return jnp.asarray(np.stack([first, bias, last]))


def kernel(x, norm_mix, w_in, attn_sink, q_a_norm, w_uq, kv_a_norm, w_ukv, w_out, final_norm):
    batch, seq, _ = x.shape
    assert norm_mix.shape[0] == 1, "single-layer block: the final norm is fused into the layer"
    assert seq % TM_PROJ == 0 and seq % TQ_LAT == 0 and seq % CK_LAT == 0
    assert seq % (QB_WIN * BLOCK) == 0 and QB_WIN % GROUP_WIN == 0 and TM_PROJ % ROWS_PROJ == 0
    cq, sq, ck, sk = _rope_tables(seq)
    bias = _window_bias()
    x2 = x.reshape(batch * seq, D_MODEL)
    w_in_p, w_uq_p, w_k_p, w_vt = _pack_weights(w_in[0], w_uq[0], w_ukv[0])
    qa, kva, gate, qb, kb, qn, kn, vbt = _input_proj(
        x2, norm_mix, w_in_p, q_a_norm, w_uq_p, kv_a_norm, w_k_p, w_vt, cq, sq, ck, sk, seq)
    sink_tab = jnp.repeat(attn_sink[0].astype(jnp.float32) * LOG2E, BLOCK).reshape(A_KV_HEADS, A_REP * BLOCK)
    yb = _latent_attn(qb, kb, vbt, gate, qn, kn, batch, seq)
    w_o = w_out[0].astype(jnp.bfloat16)
    out = _window_attn_out(qa, kva, gate, bias, sink_tab, yb, x2,
                           w_o[:A_WIDTH][A_PAIR_COLS], w_o[A_WIDTH:], final_norm[None], batch, seq)
    return out.reshape(batch, seq, D_MODEL)
```

```python
import functools

import jax
import jax.numpy as jnp
import numpy as np
from jax.experimental import pallas as pl
from jax.experimental.pallas import tpu as pltpu

D_MODEL = 1024
HEAD_DIM = 64
BLOCK = 128
A_HEADS = 8
A_KV_HEADS = 2
A_REP = A_HEADS // A_KV_HEADS
A_WIDTH = A_HEADS * HEAD_DIM
WINDOW = 128
B_HEADS = 8
B_NOPE = 64
B_ROPE = 32
B_V = 64
B_WIDTH = B_HEADS * B_V
Q_LORA = 384
KV_LORA = 256
ROPE_BASE = 10000.0
EPS = 1e-6

LANES = 128
HALF_ROPE = B_ROPE // 2
NEG = -1e30
LOG2E = float(np.log2(np.e))
QA_SCALE = HEAD_DIM ** -0.5 * LOG2E
QB_SCALE = (B_NOPE + B_ROPE) ** -0.5 * LOG2E
VMEM_LIMIT = 56 * 1024 * 1024

NOPE_A = 48

A_PAIR_COLS = np.concatenate(
    [np.arange(HEAD_DIM) + (g * A_REP + p) * HEAD_DIM for p in range(A_REP) for g in range(A_KV_HEADS)])


def _slab_cols(x1, nope, x2, zero):
    return np.concatenate([x1, nope[:NOPE_A], x2, nope[NOPE_A:],
                           np.full(LANES - B_NOPE - B_ROPE, zero)])


PER_Q = B_NOPE + B_ROPE
PER_KV = B_NOPE + B_V
UQ_SLAB_COLS = np.concatenate([
    _slab_cols(h * PER_Q + B_NOPE + np.arange(HALF_ROPE), h * PER_Q + np.arange(B_NOPE),
               h * PER_Q + B_NOPE + HALF_ROPE + np.arange(HALF_ROPE), B_HEADS * PER_Q)
    for h in range(B_HEADS)])
UK_SLAB_COLS = np.concatenate([
    _slab_cols(np.full(HALF_ROPE, B_HEADS * PER_KV), h * PER_KV + np.arange(B_NOPE),
               np.full(HALF_ROPE, B_HEADS * PER_KV), B_HEADS * PER_KV)
    for h in range(B_HEADS)])
UV_COLS = np.concatenate([h * PER_KV + B_NOPE + np.arange(B_V) for h in range(B_HEADS)])

C_QA = 0
C_KV = C_QA + A_WIDTH
C_GATE = C_KV + 2 * A_KV_HEADS * HEAD_DIM
C_CQ = C_GATE + A_WIDTH + B_WIDTH
C_CKV = C_CQ + Q_LORA
C_KR = C_CKV + KV_LORA
C_END = C_KR + LANES

TM_PROJ = 1024
ROWS_PROJ = 256
QB_WIN = 8
GROUP_WIN = 2
TQ_LAT = 512
CK_LAT = 256
BOUND_SLACK = 1.0 + 2.0 ** -10
L_FLOOR = 2.0 ** -80


def _rms(x, gain):
    return x * jax.lax.rsqrt(jnp.mean(x * x, axis=-1, keepdims=True) + EPS) * gain


def _emit_pipeline(n_groups, stages):
    state = [None] * n_groups
    for tick in range(n_groups + len(stages) - 1):
        for s in reversed(range(len(stages))):
            i = tick - s
            if 0 <= i < n_groups:
                state[i] = stages[s](i, state[i])


def _sq_norm(slab_bf16):
    v = slab_bf16.astype(jnp.float32)
    return jnp.sum(v * v, axis=-1, keepdims=True)


def _rope(slab, cos, sin):
    return slab * cos + pltpu.roll(slab, 64, 1) * sin


def _proj_kernel(x_ref, g_ref, win_ref, qg_ref, wuq_ref, kvg_ref, wk_ref, wvt_ref,
                 cq_ref, sq_ref, ck_ref, sk_ref,
                 qa_ref, kva_ref, gate_ref, qb_ref, kb_ref, qn_ref, kn_ref, vbt_ref):
    bf = jnp.bfloat16
    lane = jax.lax.broadcasted_iota(jnp.int32, (ROWS_PROJ, LANES), 1)
    rows = lambda i: pl.ds(i * ROWS_PROJ, ROWS_PROJ)

    def seg(h, lo, hi):
        return jnp.dot(h, win_ref[:, lo:hi], preferred_element_type=jnp.float32)

    def head_slabs(i, slab_of, out_ref, norm_ref):
        norms = jnp.zeros(lane.shape, jnp.float32)
        for hd in range(B_HEADS):
            slab = slab_of(slice(hd * LANES, (hd + 1) * LANES)).astype(bf)
            out_ref[hd, rows(i), :] = slab
            norms = jnp.where(lane == hd, _sq_norm(slab), norms)
        norm_ref[rows(i), :] = norms

    def s_norm(i, _):
        return _rms(x_ref[rows(i), :], g_ref[...]).astype(bf)

    def s_latent(i, h):
        cqn = _rms(seg(h, C_CQ, C_CKV), qg_ref[...]).astype(bf)
        ckvn = _rms(seg(h, C_CKV, C_KR), kvg_ref[...]).astype(bf)
        kr = _rope(seg(h, C_KR, C_END), ck_ref[rows(i), :], sk_ref[rows(i), :])
        return h, cqn, ckvn, kr

    def s_qa(i, st):
        qa_ref[rows(i), :] = (seg(st[0], C_QA, C_KV) * QA_SCALE).astype(bf)
        return st

    def s_q(i, st):
        q = jnp.dot(st[1], wuq_ref[...], preferred_element_type=jnp.float32)
        cq, sq = cq_ref[rows(i), :], sq_ref[rows(i), :]
        head_slabs(i, lambda sl: _rope(q[:, sl], cq, sq), qb_ref, qn_ref)
        return st

    def s_kva(i, st):
        kva_ref[rows(i), :] = seg(st[0], C_KV, C_GATE).astype(bf)
        return st

    def s_k(i, st):
        kn = jnp.dot(st[2], wk_ref[...], preferred_element_type=jnp.float32)
        head_slabs(i, lambda sl: kn[:, sl] + st[3], kb_ref, kn_ref)
        return st

    def s_v(i, st):
        vbt_ref[:, rows(i)] = jax.lax.dot_general(
            wvt_ref[...], st[2], (((1,), (1,)), ((), ())),
            preferred_element_type=jnp.float32).astype(bf)
        return st

    def s_gate(i, st):
        g = seg(st[0], C_GATE, C_CQ)
        gate_ref[rows(i), :] = (g / (1.0 + jnp.exp(-g))).astype(bf)
        return st

    _emit_pipeline(x_ref.shape[0] // ROWS_PROJ,
                   [s_norm, s_latent, s_qa, s_q, s_kva, s_k, s_v, s_gate])


def _input_proj(x2, norm_g, w_in_p, q_g, w_uq_p, kv_g, w_k_p, w_vt, cq, sq, ck, sk, seq):
    t = x2.shape[0]
    tm = TM_PROJ
    n_pos = seq // tm
    full = lambda a: pl.BlockSpec(a.shape, lambda i: (0,) * a.ndim)
    rows = lambda w: pl.BlockSpec((tm, w), lambda i: (i, 0))
    tab = pl.BlockSpec((tm, LANES), lambda i: (i % n_pos, 0))
    bf = jnp.bfloat16
    widths = (A_WIDTH, 2 * A_KV_HEADS * HEAD_DIM, A_WIDTH + B_WIDTH)
    heads = pl.BlockSpec((None, B_HEADS, tm, LANES), lambda i: (i // n_pos, 0, i % n_pos, 0))
    heads_shape = jax.ShapeDtypeStruct((t // seq, B_HEADS, seq, LANES), bf)
    return pl.pallas_call(
        _proj_kernel,
        grid=(t // tm,),
        in_specs=[rows(D_MODEL), full(norm_g), full(w_in_p), full(q_g), full(w_uq_p),
                  full(kv_g), full(w_k_p), full(w_vt), tab, tab, tab, tab],
        out_specs=[rows(w) for w in widths] + [heads, heads, rows(LANES), rows(LANES)]
        + [pl.BlockSpec((None, B_WIDTH, tm), lambda i: (i // n_pos, 0, i % n_pos))],
        out_shape=[jax.ShapeDtypeStruct((t, w), bf) for w in widths]
        + [heads_shape, heads_shape]
        + [jax.ShapeDtypeStruct((t, LANES), jnp.float32)] * 2
        + [jax.ShapeDtypeStruct((t // seq, B_WIDTH, seq), bf)],
        compiler_params=pltpu.CompilerParams(
            dimension_semantics=("arbitrary",), vmem_limit_bytes=VMEM_LIMIT),
        name="input_proj",
    )(x2, norm_g, w_in_p, q_g, w_uq_p, kv_g, w_k_p, w_vt, cq, sq, ck, sk)


class _WindowStages:
    def __init__(self, refs, step, n_blocks):
        (self.q_ref, self.kva_ref, self.gate_ref, self.bias_ref, self.sink_ref, self.yb_ref,
         self.x_ref, self.wa_ref, self.wb_ref, self.g_ref, self.o_ref, self.vt_ref,
         self.ya_ref) = refs
        self.step = step
        self.n_blocks = n_blocks
        self.first = jax.lax.broadcasted_iota(jnp.int32, (BLOCK, LANES), 1) < HEAD_DIM

    def _band(self, blk):
        n = self.step * QB_WIN + blk
        return (jnp.maximum(n - 1, 0), n, jnp.minimum(n + 1, self.n_blocks - 1))

    def _bias_idx(self, blk):
        if blk == 0:
            return jnp.where(self.step == 0, 0, 1)
        if blk == QB_WIN - 1:
            return jnp.where(self.step == self.n_blocks // QB_WIN - 1, 2, 1)
        return 1

    def scores(self, blk):
        rows = pl.ds(blk * BLOCK, BLOCK)
        kk = jnp.concatenate(
            [self.kva_ref[pl.ds(pl.multiple_of(b * BLOCK, BLOCK), BLOCK), :LANES]
             for b in self._band(blk)], axis=0)
        qp = [self.q_ref[rows, p * LANES:(p + 1) * LANES] for p in range(A_REP)]
        zero = jnp.zeros((), jnp.bfloat16)
        out = []
        for g in range(A_KV_HEADS):
            keep = self.first if g == 0 else jnp.logical_not(self.first)
            q4 = jnp.concatenate([jnp.where(keep, x, zero) for x in qp], axis=0)
            st = jax.lax.dot_general(kk, q4, (((1,), (1,)), ((), ())),
                                     preferred_element_type=jnp.float32)
            out.append(st + self.bias_ref[self._bias_idx(blk), g])
        return out

    def softmax(self, sts):
        out = []
        for g, st in enumerate(sts):
            sink = self.sink_ref[g:g + 1, :]
            m = jnp.maximum(jnp.max(st, axis=0, keepdims=True), sink)
            p = jnp.exp2(st - m)
            inv = 1.0 / (jnp.sum(p, axis=0, keepdims=True) + jnp.exp2(sink - m))
            out.append((p.astype(jnp.bfloat16), inv))
        return out

    def values(self, blk, probs):
        rows = pl.ds(blk * BLOCK, BLOCK)
        vts = [self.vt_ref[b] for b in self._band(blk)]
        outs = []
        for g, (pb, inv) in enumerate(probs):
            vt = jnp.concatenate([x[g * HEAD_DIM:(g + 1) * HEAD_DIM, :] for x in vts], axis=1)
            outs.append(jnp.dot(vt, pb, preferred_element_type=jnp.float32) * inv)
        for pr in range(A_REP):
            sl = slice(pr * LANES, (pr + 1) * LANES)
            o = jnp.concatenate([outs[0][:, sl], outs[1][:, sl]], axis=0).T
            self.ya_ref[rows, sl] = (o * self.gate_ref[rows, sl].astype(jnp.float32)
                                     ).astype(jnp.bfloat16)

    def out_proj(self, rows):
        d = jnp.dot(self.ya_ref[rows, :], self.wa_ref[...], preferred_element_type=jnp.float32)
        d = d + jnp.dot(self.yb_ref[rows, :], self.wb_ref[...], preferred_element_type=jnp.float32)
        return self.x_ref[rows, :] + d

    def norm_store(self, rows, y):
        self.o_ref[rows, :] = _rms(y, self.g_ref[...])


def _window_out_kernel(*refs, n_blocks):
    kva_ref, vt_ref = refs[1], refs[11]
    step = pl.program_id(1)

    @pl.when(step == 0)
    def _():
        for b in range(n_blocks):
            v = kva_ref[b * BLOCK:(b + 1) * BLOCK, LANES:].astype(jnp.float32)
            vt_ref[b] = v.T.astype(jnp.bfloat16)

    st = _WindowStages(refs, step, n_blocks)
    groups = [list(range(g, g + GROUP_WIN)) for g in range(0, QB_WIN, GROUP_WIN)]
    rows = [pl.ds(g[0] * BLOCK, len(g) * BLOCK) for g in groups]
    _emit_pipeline(len(groups), [
        lambda i, _: [st.scores(b) for b in groups[i]],
        lambda i, scores: [st.softmax(s) for s in scores],
        lambda i, probs: [st.values(b, p) for b, p in zip(groups[i], probs)],
        lambda i, _: st.out_proj(rows[i]),
        lambda i, y: st.norm_store(rows[i], y),
    ])


def _window_attn_out(qa, kva, gate, bias, sink_tab, yb, x2, w_a, w_b, final_g, batch, seq):
    nb = seq // BLOCK
    ns = nb // QB_WIN
    tq = QB_WIN * BLOCK
    rows = lambda w: pl.BlockSpec((tq, w), lambda b, n: (b * ns + n, 0))
    full = lambda a: pl.BlockSpec(a.shape, lambda b, n: (0,) * a.ndim)
    return pl.pallas_call(
        functools.partial(_window_out_kernel, n_blocks=nb),
        grid=(batch, ns),
        in_specs=[rows(A_WIDTH),
                  pl.BlockSpec((seq, 2 * A_KV_HEADS * HEAD_DIM), lambda b, n: (b, 0)),
                  rows(A_WIDTH), full(bias), full(sink_tab), rows(B_WIDTH), rows(D_MODEL),
                  full(w_a), full(w_b), full(final_g)],
        out_specs=rows(D_MODEL),
        out_shape=jax.ShapeDtypeStruct((batch * seq, D_MODEL), jnp.float32),
        scratch_shapes=[pltpu.VMEM((nb, LANES, BLOCK), jnp.bfloat16),
                        pltpu.VMEM((tq, A_WIDTH), jnp.bfloat16)],
        compiler_params=pltpu.CompilerParams(
            dimension_semantics=("arbitrary", "arbitrary"), vmem_limit_bytes=VMEM_LIMIT),
        name="window_attn_out",
    )(qa, kva, gate, bias, sink_tab, yb, x2, w_a, w_b, final_g)


def _latent_kernel(q_ref, k_ref, vt_ref, gate_ref, qn_ref, kn_ref, o_ref):
    n_chunks = k_ref.shape[1] // CK_LAT

    def scores(hd):
        return jax.lax.dot_general(k_ref[hd], q_ref[hd], (((1,), (1,)), ((), ())),
                                   preferred_element_type=jnp.float32)

    def weighted_values(hd, st, shift):
        l = acc = None
        for c in range(n_chunks):
            keys = slice(c * CK_LAT, (c + 1) * CK_LAT)
            p = jnp.exp2(st[keys] - shift)
            lc = jnp.sum(p, axis=0, keepdims=True)
            ac = jnp.dot(vt_ref[hd * B_V:(hd + 1) * B_V, keys], p.astype(jnp.bfloat16),
                         preferred_element_type=jnp.float32)
            l = lc if c == 0 else l + lc
            acc = ac if c == 0 else acc + ac
        return acc, l

    def store_pair(hd, outs):
        o = jnp.concatenate(outs, axis=0).T
        sl = slice((hd // 2) * LANES, (hd // 2 + 1) * LANES)
        o_ref[:, sl] = (o * gate_ref[:, sl].astype(jnp.float32)).astype(jnp.bfloat16)

    k_norm2 = jnp.max(kn_ref[...], axis=0, keepdims=True)
    q_norm2 = qn_ref[...].T
    outs, l_min = [], None
    for hd in range(B_HEADS):
        shift = jnp.sqrt(q_norm2[hd:hd + 1, :] * k_norm2[:, hd:hd + 1]) * BOUND_SLACK
        acc, l = weighted_values(hd, scores(hd), shift)
        l_min = l if hd == 0 else jnp.minimum(l_min, l)
        outs.append(acc * (1.0 / l))
        if hd % 2 == 1:
            store_pair(hd, outs)
            outs = []

    @pl.when(jnp.logical_not(jnp.min(l_min) >= L_FLOOR))
    def _():
        outs = []
        nxt = scores(0)
        for hd in range(B_HEADS):
            st = nxt
            if hd + 1 < B_HEADS:
                nxt = scores(hd + 1)
            acc, l = weighted_values(hd, st, jnp.max(st, axis=0, keepdims=True))
            outs.append(acc * (1.0 / l))
            if hd % 2 == 1:
                store_pair(hd, outs)
                outs = []


def _latent_attn(qb, kb, vbt, gate, qn, kn, batch, seq):
    tq = TQ_LAT
    nq = seq // tq
    return pl.pallas_call(
        _latent_kernel,
        grid=(batch, nq),
        in_specs=[pl.BlockSpec((None, B_HEADS, tq, LANES), lambda b, i: (b, 0, i, 0)),
                  pl.BlockSpec((None, B_HEADS, seq, LANES), lambda b, i: (b, 0, 0, 0)),
                  pl.BlockSpec((None, B_WIDTH, seq), lambda b, i: (b, 0, 0)),
                  pl.BlockSpec((tq, B_WIDTH), lambda b, i: (b * nq + i, 1)),
                  pl.BlockSpec((tq, LANES), lambda b, i: (b * nq + i, 0)),
                  pl.BlockSpec((seq, LANES), lambda b, i: (b, 0))],
        out_specs=pl.BlockSpec((tq, B_WIDTH), lambda b, i: (b * nq + i, 0)),
        out_shape=jax.ShapeDtypeStruct((batch * seq, B_WIDTH), jnp.bfloat16),
        compiler_params=pltpu.CompilerParams(
            dimension_semantics=("arbitrary", "arbitrary"), vmem_limit_bytes=VMEM_LIMIT),
        name="latent_attn",
    )(qb, kb, vbt, gate, qn, kn)


def _head_slab(x1, nope, x2):
    z = jnp.zeros((nope.shape[0], LANES - B_NOPE - B_ROPE), nope.dtype)
    return jnp.concatenate([x1, nope[:, :NOPE_A], x2, nope[:, NOPE_A:], z], axis=1)


def _pack_weights(w_in, w_uq, w_ukv):
    bf = jnp.bfloat16
    w_in, w_uq, w_ukv = w_in.astype(bf), w_uq.astype(bf), w_ukv.astype(bf)
    sizes = (A_WIDTH, A_KV_HEADS * HEAD_DIM, A_KV_HEADS * HEAD_DIM, A_WIDTH,
             Q_LORA, KV_LORA, B_ROPE, B_WIDTH)
    offs = np.cumsum((0,) + sizes)
    qa, ka, va, ga, cq, ckv, kr, gb = (w_in[:, offs[i]:offs[i + 1]] for i in range(8))
    zr = jnp.zeros((w_in.shape[0], B_NOPE), w_in.dtype)
    kr_slab = _head_slab(kr[:, :HALF_ROPE], zr, kr[:, HALF_ROPE:])
    w_in_p = jnp.concatenate(
        [qa[:, A_PAIR_COLS], ka, va, ga[:, A_PAIR_COLS], gb, cq, ckv, kr_slab], axis=1)

    pad = lambda w: jnp.pad(w, ((0, 0), (0, 1)))
    w_uq_p = jnp.take(pad(w_uq), UQ_SLAB_COLS, axis=1)
    w_ukv = pad(w_ukv)
    return (w_in_p, w_uq_p, jnp.take(w_ukv, UK_SLAB_COLS, axis=1),
            jnp.take(w_ukv, UV_COLS, axis=1).T)


def _rope_tables(seq):
    pos = np.arange(seq, dtype=np.float32)
    inv_freq = (np.float32(ROPE_BASE) ** (-np.arange(0, B_ROPE, 2, dtype=np.float32) / B_ROPE)
                ).astype(np.float32)
    ang = pos[:, None] * inv_freq[None, :]
    cos, sin = np.cos(ang), np.sin(ang)
    one = np.ones((seq, B_NOPE), np.float32)
    zero = np.zeros((seq, B_NOPE), np.float32)
    pad = np.zeros((seq, LANES - B_NOPE - B_ROPE), np.float32)
    c = np.concatenate([cos, one[:, :NOPE_A], cos, one[:, NOPE_A:], pad], axis=1)
    s = np.concatenate([-sin, zero[:, :NOPE_A], sin, zero[:, NOPE_A:], pad], axis=1)
    qs = np.float32(QB_SCALE)
    return tuple(jnp.asarray(t.astype(np.float32)) for t in (c * qs, s * qs, c, s))


def _window_bias():
    kj = np.arange(3 * BLOCK)[:, None]
    qi = np.arange(BLOCK)[None, :]
    dist = np.abs(qi - kj + BLOCK).astype(np.float32)
    slopes = np.exp2(-8.0 * np.arange(1, A_HEADS + 1, dtype=np.float32) / A_HEADS)
    bias = -slopes[:, None, None] * dist[None] * LOG2E
    bias = np.where((dist <= WINDOW)[None], bias, NEG).astype(np.float32)
    bias = bias.reshape(A_KV_HEADS, A_REP, 3 * BLOCK, BLOCK).transpose(0, 2, 1, 3)
    bias = bias.reshape(A_KV_HEADS, 3 * BLOCK, A_REP * BLOCK)
    first, last = bias.copy(), bias.copy()
    first[:, :BLOCK] = NEG
    last[:, 2 * BLOCK:] = NEG
    return jnp.asarray(np.stack([first, bias, last]))


def kernel(x, norm_mix, w_in, attn_sink, q_a_norm, w_uq, kv_a_norm, w_ukv, w_out, final_norm):
    batch, seq, _ = x.shape
    assert norm_mix.shape[0] == 1, "single-layer block: the final norm is fused into the layer"
    assert seq % TM_PROJ == 0 and seq % TQ_LAT == 0 and seq % CK_LAT == 0
    assert seq % (QB_WIN * BLOCK) == 0 and QB_WIN % GROUP_WIN == 0 and TM_PROJ % ROWS_PROJ == 0
    cq, sq, ck, sk = _rope_tables(seq)
    bias = _window_bias()
    x2 = x.reshape(batch * seq, D_MODEL)
    w_in_p, w_uq_p, w_k_p, w_vt = _pack_weights(w_in[0], w_uq[0], w_ukv[0])
    qa, kva, gate, qb, kb, qn, kn, vbt = _input_proj(
        x2, norm_mix, w_in_p, q_a_norm, w_uq_p, kv_a_norm, w_k_p, w_vt, cq, sq, ck, sk, seq)
    sink_tab = jnp.repeat(attn_sink[0].astype(jnp.float32) * LOG2E, BLOCK).reshape(A_KV_HEADS, A_REP * BLOCK)
    yb = _latent_attn(qb, kb, vbt, gate, qn, kn, batch, seq)
    w_o = w_out[0].astype(jnp.bfloat16)
    out = _window_attn_out(qa, kva, gate, bias, sink_tab, yb, x2,
                           w_o[:A_WIDTH][A_PAIR_COLS], w_o[A_WIDTH:], final_norm[None], batch, seq)
    return out.reshape(batch, seq, D_MODEL)
```

```python
import functools

import jax
import jax.numpy as jnp
import numpy as np
from jax.experimental import pallas as pl
from jax.experimental.pallas import tpu as pltpu

D_MODEL = 1024
HEAD_DIM = 64
BLOCK = 128
A_HEADS = 8
A_KV_HEADS = 2
A_REP = A_HEADS // A_KV_HEADS
A_WIDTH = A_HEADS * HEAD_DIM
WINDOW = 128
B_HEADS = 8
B_NOPE = 64
B_ROPE = 32
B_V = 64
B_WIDTH = B_HEADS * B_V
Q_LORA = 384
KV_LORA = 256
ROPE_BASE = 10000.0
EPS = 1e-6

LANES = 128
HALF_ROPE = B_ROPE // 2
NEG = -1e30
LOG2E = float(np.log2(np.e))
QA_SCALE = HEAD_DIM ** -0.5 * LOG2E
QB_SCALE = (B_NOPE + B_ROPE) ** -0.5 * LOG2E
VMEM_LIMIT = 56 * 1024 * 1024

NOPE_A = 48

A_PAIR_COLS = np.concatenate(
    [np.arange(HEAD_DIM) + (g * A_REP + p) * HEAD_DIM for p in range(A_REP) for g in range(A_KV_HEADS)])


def _slab_cols(x1, nope, x2, zero):
    return np.concatenate([x1, nope[:NOPE_A], x2, nope[NOPE_A:],
                           np.full(LANES - B_NOPE - B_ROPE, zero)])


PER_Q = B_NOPE + B_ROPE
PER_KV = B_NOPE + B_V
UQ_SLAB_COLS = np.concatenate([
    _slab_cols(h * PER_Q + B_NOPE + np.arange(HALF_ROPE), h * PER_Q + np.arange(B_NOPE),
               h * PER_Q + B_NOPE + HALF_ROPE + np.arange(HALF_ROPE), B_HEADS * PER_Q)
    for h in range(B_HEADS)])
UK_SLAB_COLS = np.concatenate([
    _slab_cols(np.full(HALF_ROPE, B_HEADS * PER_KV), h * PER_KV + np.arange(B_NOPE),
               np.full(HALF_ROPE, B_HEADS * PER_KV), B_HEADS * PER_KV)
    for h in range(B_HEADS)])
UV_COLS = np.concatenate([h * PER_KV + B_NOPE + np.arange(B_V) for h in range(B_HEADS)])

TM_PROJ = 1024
ROWS_PROJ = 256
QB_WIN = 8
GROUP_WIN = 2
TQ_LAT = 512
CK_LAT = 256
BOUND_SLACK = 1.0 + 2.0 ** -10
L_FLOOR = 2.0 ** -80


def _rms(x, gain):
    return x * jax.lax.rsqrt(jnp.mean(x * x, axis=-1, keepdims=True) + EPS) * gain


def _emit_pipeline(n_groups, stages):
    state = [None] * n_groups
    for tick in range(n_groups + len(stages) - 1):
        for s in reversed(range(len(stages))):
            i = tick - s
            if 0 <= i < n_groups:
                state[i] = stages[s](i, state[i])


def _sq_norm(slab_bf16):
    v = slab_bf16.astype(jnp.float32)
    return jnp.sum(v * v, axis=-1, keepdims=True)


def _rope(slab, cos, sin):
    return slab * cos + pltpu.roll(slab, 64, 1) * sin


def _proj_kernel(x_ref, g_ref, wqa_ref, wkv_ref, wga_ref, wgb_ref, wlat_ref, wkr_ref,
                 qg_ref, wuq_ref, kvg_ref, wk_ref, wvt_ref, cq_ref, sq_ref, ck_ref, sk_ref,
                 qa_ref, kva_ref, gate_ref, qb_ref, kb_ref, qn_ref, kn_ref, vbt_ref):
    bf = jnp.bfloat16
    lane = jax.lax.broadcasted_iota(jnp.int32, (ROWS_PROJ, LANES), 1)
    rows = lambda i: pl.ds(i * ROWS_PROJ, ROWS_PROJ)
    wkv, wgb, wlat = (r[...].astype(bf) for r in (wkv_ref, wgb_ref, wlat_ref))

    def seg(h, w):
        return jnp.dot(h, w, preferred_element_type=jnp.float32)

    def head_slabs(i, slab_of, out_ref, norm_ref):
        norms = jnp.zeros(lane.shape, jnp.float32)
        for hd in range(B_HEADS):
            slab = slab_of(slice(hd * LANES, (hd + 1) * LANES)).astype(bf)
            out_ref[hd, rows(i), :] = slab
            norms = jnp.where(lane == hd, _sq_norm(slab), norms)
        norm_ref[rows(i), :] = norms

    def s_norm(i, _):
        return _rms(x_ref[rows(i), :], g_ref[...]).astype(bf)

    def s_latent(i, h):
        cqn = _rms(seg(h, wlat[:, :Q_LORA]), qg_ref[...]).astype(bf)
        ckvn = _rms(seg(h, wlat[:, Q_LORA:]), kvg_ref[...]).astype(bf)
        kr = _rope(seg(h, wkr_ref[...]), ck_ref[rows(i), :], sk_ref[rows(i), :])
        return h, cqn, ckvn, kr

    def s_qa(i, st):
        qa_ref[rows(i), :] = (seg(st[0], wqa_ref[...]) * QA_SCALE).astype(bf)
        return st

    def s_q(i, st):
        q = jnp.dot(st[1], wuq_ref[...], preferred_element_type=jnp.float32)
        cq, sq = cq_ref[rows(i), :], sq_ref[rows(i), :]
        head_slabs(i, lambda sl: _rope(q[:, sl], cq, sq), qb_ref, qn_ref)
        return st

    def s_kva(i, st):
        kva_ref[rows(i), :] = seg(st[0], wkv).astype(bf)
        return st

    def s_k(i, st):
        kn = jnp.dot(st[2], wk_ref[...], preferred_element_type=jnp.float32)
        head_slabs(i, lambda sl: kn[:, sl] + st[3], kb_ref, kn_ref)
        return st

    def s_v(i, st):
        vbt_ref[:, rows(i)] = jax.lax.dot_general(
            wvt_ref[...], st[2], (((1,), (1,)), ((), ())),
            preferred_element_type=jnp.float32).astype(bf)
        return st

    def s_gate(i, st):
        for cols, w in ((slice(0, A_WIDTH), wga_ref[...]), (slice(A_WIDTH, A_WIDTH + B_WIDTH), wgb)):
            g = seg(st[0], w)
            gate_ref[rows(i), cols] = (g / (1.0 + jnp.exp(-g))).astype(bf)
        return st

    _emit_pipeline(x_ref.shape[0] // ROWS_PROJ,
                   [s_norm, s_latent, s_qa, s_q, s_kva, s_k, s_v, s_gate])


def _input_proj(x2, norm_g, w_in_parts, q_g, w_uq_p, kv_g, w_k_p, w_vt, cq, sq, ck, sk, seq):
    t = x2.shape[0]
    tm = TM_PROJ
    n_pos = seq // tm
    full = lambda a: pl.BlockSpec(a.shape, lambda i: (0,) * a.ndim)
    rows = lambda w: pl.BlockSpec((tm, w), lambda i: (i, 0))
    tab = pl.BlockSpec((tm, LANES), lambda i: (i % n_pos, 0))
    bf = jnp.bfloat16
    widths = (A_WIDTH, 2 * A_KV_HEADS * HEAD_DIM, A_WIDTH + B_WIDTH)
    heads = pl.BlockSpec((None, B_HEADS, tm, LANES), lambda i: (i // n_pos, 0, i % n_pos, 0))
    heads_shape = jax.ShapeDtypeStruct((t // seq, B_HEADS, seq, LANES), bf)
    return pl.pallas_call(
        _proj_kernel,
        grid=(t // tm,),
        in_specs=[rows(D_MODEL), full(norm_g)] + [full(w) for w in w_in_parts]
        + [full(q_g), full(w_uq_p), full(kv_g), full(w_k_p), full(w_vt), tab, tab, tab, tab],
        out_specs=[rows(w) for w in widths] + [heads, heads, rows(LANES), rows(LANES)]
        + [pl.BlockSpec((None, B_WIDTH, tm), lambda i: (i // n_pos, 0, i % n_pos))],
        out_shape=[jax.ShapeDtypeStruct((t, w), bf) for w in widths]
        + [heads_shape, heads_shape]
        + [jax.ShapeDtypeStruct((t, LANES), jnp.float32)] * 2
        + [jax.ShapeDtypeStruct((t // seq, B_WIDTH, seq), bf)],
        compiler_params=pltpu.CompilerParams(
            dimension_semantics=("arbitrary",), vmem_limit_bytes=VMEM_LIMIT),
        name="input_proj",
    )(x2, norm_g, *w_in_parts, q_g, w_uq_p, kv_g, w_k_p, w_vt, cq, sq, ck, sk)


class _WindowStages:
    def __init__(self, refs, step, n_blocks):
        (self.q_ref, self.kva_ref, self.gate_ref, self.bias_ref, self.sink_ref, self.yb_ref,
         self.x_ref, self.wa_ref, self.wb_ref, self.g_ref, self.o_ref, self.vt_ref,
         self.ya_ref) = refs
        self.step = step
        self.n_blocks = n_blocks
        self.first = jax.lax.broadcasted_iota(jnp.int32, (BLOCK, LANES), 1) < HEAD_DIM

    def _band(self, blk):
        n = self.step * QB_WIN + blk
        return (jnp.maximum(n - 1, 0), n, jnp.minimum(n + 1, self.n_blocks - 1))

    def _bias_idx(self, blk):
        if blk == 0:
            return jnp.where(self.step == 0, 0, 1)
        if blk == QB_WIN - 1:
            return jnp.where(self.step == self.n_blocks // QB_WIN - 1, 2, 1)
        return 1

    def scores(self, blk):
        rows = pl.ds(blk * BLOCK, BLOCK)
        kk = jnp.concatenate(
            [self.kva_ref[pl.ds(pl.multiple_of(b * BLOCK, BLOCK), BLOCK), :LANES]
             for b in self._band(blk)], axis=0)
        qp = [self.q_ref[rows, p * LANES:(p + 1) * LANES] for p in range(A_REP)]
        zero = jnp.zeros((), jnp.bfloat16)
        out = []
        for g in range(A_KV_HEADS):
            keep = self.first if g == 0 else jnp.logical_not(self.first)
            q4 = jnp.concatenate([jnp.where(keep, x, zero) for x in qp], axis=0)
            st = jax.lax.dot_general(kk, q4, (((1,), (1,)), ((), ())),
                                     preferred_element_type=jnp.float32)
            out.append(st + self.bias_ref[self._bias_idx(blk), g])
        return out

    def softmax(self, sts):
        out = []
        for g, st in enumerate(sts):
            sink = self.sink_ref[g:g + 1, :]
            m = jnp.maximum(jnp.max(st, axis=0, keepdims=True), sink)
            p = jnp.exp2(st - m)
            inv = 1.0 / (jnp.sum(p, axis=0, keepdims=True) + jnp.exp2(sink - m))
            out.append((p.astype(jnp.bfloat16), inv))
        return out

    def values(self, blk, probs):
        rows = pl.ds(blk * BLOCK, BLOCK)
        vts = [self.vt_ref[b] for b in self._band(blk)]
        outs = []
        for g, (pb, inv) in enumerate(probs):
            vt = jnp.concatenate([x[g * HEAD_DIM:(g + 1) * HEAD_DIM, :] for x in vts], axis=1)
            outs.append(jnp.dot(vt, pb, preferred_element_type=jnp.float32) * inv)
        for pr in range(A_REP):
            sl = slice(pr * LANES, (pr + 1) * LANES)
            o = jnp.concatenate([outs[0][:, sl], outs[1][:, sl]], axis=0).T
            self.ya_ref[rows, sl] = (o * self.gate_ref[rows, sl].astype(jnp.float32)
                                     ).astype(jnp.bfloat16)

    def out_proj(self, rows):
        d = jnp.dot(self.ya_ref[rows, :], self.wa_ref[...], preferred_element_type=jnp.float32)
        d = d + jnp.dot(self.yb_ref[rows, :], self.wb_ref[...], preferred_element_type=jnp.float32)
        return self.x_ref[rows, :] + d

    def norm_store(self, rows, y):
        self.o_ref[rows, :] = _rms(y, self.g_ref[...])


def _window_out_kernel(*refs, n_blocks):
    kva_ref, vt_ref = refs[1], refs[11]
    step = pl.program_id(1)

    @pl.when(step == 0)
    def _():
        for b in range(n_blocks):
            v = kva_ref[b * BLOCK:(b + 1) * BLOCK, LANES:].astype(jnp.float32)
            vt_ref[b] = v.T.astype(jnp.bfloat16)

    st = _WindowStages(refs, step, n_blocks)
    groups = [list(range(g, g + GROUP_WIN)) for g in range(0, QB_WIN, GROUP_WIN)]
    rows = [pl.ds(g[0] * BLOCK, len(g) * BLOCK) for g in groups]
    _emit_pipeline(len(groups), [
        lambda i, _: [st.scores(b) for b in groups[i]],
        lambda i, scores: [st.softmax(s) for s in scores],
        lambda i, probs: [st.values(b, p) for b, p in zip(groups[i], probs)],
        lambda i, _: st.out_proj(rows[i]),
        lambda i, y: st.norm_store(rows[i], y),
    ])


def _window_attn_out(qa, kva, gate, bias, sink_tab, yb, x2, w_a, w_b, final_g, batch, seq):
    nb = seq // BLOCK
    ns = nb // QB_WIN
    tq = QB_WIN * BLOCK
    rows = lambda w: pl.BlockSpec((tq, w), lambda b, n: (b * ns + n, 0))
    full = lambda a: pl.BlockSpec(a.shape, lambda b, n: (0,) * a.ndim)
    return pl.pallas_call(
        functools.partial(_window_out_kernel, n_blocks=nb),
        grid=(batch, ns),
        in_specs=[rows(A_WIDTH),
                  pl.BlockSpec((seq, 2 * A_KV_HEADS * HEAD_DIM), lambda b, n: (b, 0)),
                  rows(A_WIDTH), full(bias), full(sink_tab), rows(B_WIDTH), rows(D_MODEL),
                  full(w_a), full(w_b), full(final_g)],
        out_specs=rows(D_MODEL),
        out_shape=jax.ShapeDtypeStruct((batch * seq, D_MODEL), jnp.float32),
        scratch_shapes=[pltpu.VMEM((nb, LANES, BLOCK), jnp.bfloat16),
                        pltpu.VMEM((tq, A_WIDTH), jnp.bfloat16)],
        compiler_params=pltpu.CompilerParams(
            dimension_semantics=("arbitrary", "arbitrary"), vmem_limit_bytes=VMEM_LIMIT),
        name="window_attn_out",
    )(qa, kva, gate, bias, sink_tab, yb, x2, w_a, w_b, final_g)


def _latent_kernel(q_ref, k_ref, vt_ref, gate_ref, qn_ref, kn_ref, o_ref):
    n_chunks = k_ref.shape[1] // CK_LAT

    def scores(hd):
        return jax.lax.dot_general(k_ref[hd], q_ref[hd], (((1,), (1,)), ((), ())),
                                   preferred_element_type=jnp.float32)

    def weighted_values(hd, st, shift):
        l = acc = None
        for c in range(n_chunks):
            keys = slice(c * CK_LAT, (c + 1) * CK_LAT)
            p = jnp.exp2(st[keys] - shift)
            lc = jnp.sum(p, axis=0, keepdims=True)
            ac = jnp.dot(vt_ref[hd * B_V:(hd + 1) * B_V, keys], p.astype(jnp.bfloat16),
                         preferred_element_type=jnp.float32)
            l = lc if c == 0 else l + lc
            acc = ac if c == 0 else acc + ac
        return acc, l

    def store_pair(hd, outs):
        o = jnp.concatenate(outs, axis=0).T
        sl = slice((hd // 2) * LANES, (hd // 2 + 1) * LANES)
        o_ref[:, sl] = (o * gate_ref[:, sl].astype(jnp.float32)).astype(jnp.bfloat16)

    k_norm2 = jnp.max(kn_ref[...], axis=0, keepdims=True)
    q_norm2 = qn_ref[...].T
    outs, l_min = [], None
    for hd in range(B_HEADS):
        shift = jnp.sqrt(q_norm2[hd:hd + 1, :] * k_norm2[:, hd:hd + 1]) * BOUND_SLACK
        acc, l = weighted_values(hd, scores(hd), shift)
        l_min = l if hd == 0 else jnp.minimum(l_min, l)
        outs.append(acc * (1.0 / l))
        if hd % 2 == 1:
            store_pair(hd, outs)
            outs = []

    @pl.when(jnp.logical_not(jnp.min(l_min) >= L_FLOOR))
    def _():
        outs = []
        nxt = scores(0)
        for hd in range(B_HEADS):
            st = nxt
            if hd + 1 < B_HEADS:
                nxt = scores(hd + 1)
            acc, l = weighted_values(hd, st, jnp.max(st, axis=0, keepdims=True))
            outs.append(acc * (1.0 / l))
            if hd % 2 == 1:
                store_pair(hd, outs)
                outs = []


def _latent_attn(qb, kb, vbt, gate, qn, kn, batch, seq):
    tq = TQ_LAT
    nq = seq // tq
    return pl.pallas_call(
        _latent_kernel,
        grid=(batch, nq),
        in_specs=[pl.BlockSpec((None, B_HEADS, tq, LANES), lambda b, i: (b, 0, i, 0)),
                  pl.BlockSpec((None, B_HEADS, seq, LANES), lambda b, i: (b, 0, 0, 0)),
                  pl.BlockSpec((None, B_WIDTH, seq), lambda b, i: (b, 0, 0)),
                  pl.BlockSpec((tq, B_WIDTH), lambda b, i: (b * nq + i, 1)),
                  pl.BlockSpec((tq, LANES), lambda b, i: (b * nq + i, 0)),
                  pl.BlockSpec((seq, LANES), lambda b, i: (b, 0))],
        out_specs=pl.BlockSpec((tq, B_WIDTH), lambda b, i: (b * nq + i, 0)),
        out_shape=jax.ShapeDtypeStruct((batch * seq, B_WIDTH), jnp.bfloat16),
        compiler_params=pltpu.CompilerParams(
            dimension_semantics=("arbitrary", "arbitrary"), vmem_limit_bytes=VMEM_LIMIT),
        name="latent_attn",
    )(qb, kb, vbt, gate, qn, kn)


def _head_slab(x1, nope, x2):
    z = jnp.zeros((nope.shape[0], LANES - B_NOPE - B_ROPE), nope.dtype)
    return jnp.concatenate([x1, nope[:, :NOPE_A], x2, nope[:, NOPE_A:], z], axis=1)


def _pack_weights(w_in, w_uq, w_ukv):
    bf = jnp.bfloat16
    w_uq, w_ukv = w_uq.astype(bf), w_ukv.astype(bf)
    sizes = (A_WIDTH, A_KV_HEADS * HEAD_DIM, A_KV_HEADS * HEAD_DIM, A_WIDTH,
             Q_LORA, KV_LORA, B_ROPE, B_WIDTH)
    offs = np.cumsum((0,) + sizes)
    part = lambda a, b: w_in[:, offs[a]:offs[b]]
    kr = part(6, 7).astype(bf)
    zr = jnp.zeros((w_in.shape[0], B_NOPE), bf)
    w_in_parts = (part(0, 1).astype(bf)[:, A_PAIR_COLS], part(1, 3), part(3, 4).astype(bf)[:, A_PAIR_COLS],
                  part(7, 8), part(4, 6), _head_slab(kr[:, :HALF_ROPE], zr, kr[:, HALF_ROPE:]))

    pad = lambda w: jnp.pad(w, ((0, 0), (0, 1)))
    w_uq_p = jnp.take(pad(w_uq), UQ_SLAB_COLS, axis=1)
    w_ukv = pad(w_ukv)
    return (w_in_parts, w_uq_p, jnp.take(w_ukv, UK_SLAB_COLS, axis=1),
            jnp.take(w_ukv, UV_COLS, axis=1).T)


def _rope_tables(seq):
    pos = np.arange(seq, dtype=np.float32)
    inv_freq = (np.float32(ROPE_BASE) ** (-np.arange(0, B_ROPE, 2, dtype=np.float32) / B_ROPE)
                ).astype(np.float32)
    ang = pos[:, None] * inv_freq[None, :]
    cos, sin = np.cos(ang), np.sin(ang)
    one = np.ones((seq, B_NOPE), np.float32)
    zero = np.zeros((seq, B_NOPE), np.float32)
    pad = np.zeros((seq, LANES - B_NOPE - B_ROPE), np.float32)
    c = np.concatenate([cos, one[:, :NOPE_A], cos, one[:, NOPE_A:], pad], axis=1)
    s = np.concatenate([-sin, zero[:, :NOPE_A], sin, zero[:, NOPE_A:], pad], axis=1)
    qs = np.float32(QB_SCALE)
    return tuple(jnp.asarray(t.astype(np.float32)) for t in (c * qs, s * qs, c, s))


def _window_bias():
    kj = np.arange(3 * BLOCK)[:, None]
    qi = np.arange(BLOCK)[None, :]
    dist = np.abs(qi - kj + BLOCK).astype(np.float32)
    slopes = np.exp2(-8.0 * np.arange(1, A_HEADS + 1, dtype=np.float32) / A_HEADS)
    bias = -slopes[:, None, None] * dist[None] * LOG2E
    bias = np.where((dist <= WINDOW)[None], bias, NEG).astype(np.float32)
    bias = bias.reshape(A_KV_HEADS, A_REP, 3 * BLOCK, BLOCK).transpose(0, 2, 1, 3)
    bias = bias.reshape(A_KV_HEADS, 3 * BLOCK, A_REP * BLOCK)
    first, last = bias.copy(), bias.copy()
    first[:, :BLOCK] = NEG
    last[:, 2 * BLOCK:] = NEG
    return jnp.asarray(np.stack([first, bias, last]))


def kernel(x, norm_mix, w_in, attn_sink, q_a_norm, w_uq, kv_a_norm, w_ukv, w_out, final_norm):
    batch, seq, _ = x.shape
    assert norm_mix.shape[0] == 1, "single-layer block: the final norm is fused into the layer"
    assert seq % TM_PROJ == 0 and seq % TQ_LAT == 0 and seq % CK_LAT == 0
    assert seq % (QB_WIN * BLOCK) == 0 and QB_WIN % GROUP_WIN == 0 and TM_PROJ % ROWS_PROJ == 0
    cq, sq, ck, sk = _rope_tables(seq)
    bias = _window_bias()
    x2 = x.reshape(batch * seq, D_MODEL)
    w_in_parts, w_uq_p, w_k_p, w_vt = _pack_weights(w_in[0], w_uq[0], w_ukv[0])
    qa, kva, gate, qb, kb, qn, kn, vbt = _input_proj(
        x2, norm_mix, w_in_parts, q_a_norm, w_uq_p, kv_a_norm, w_k_p, w_vt, cq, sq, ck, sk, seq)
    sink_tab = jnp.repeat(attn_sink[0].astype(jnp.float32) * LOG2E, BLOCK).reshape(A_KV_HEADS, A_REP * BLOCK)
    yb = _latent_attn(qb, kb, vbt, gate, qn, kn, batch, seq)
    w_o = w_out[0].astype(jnp.bfloat16)
    out = _window_attn_out(qa, kva, gate, bias, sink_tab, yb, x2,
                           w_o[:A_WIDTH][A_PAIR_COLS], w_o[A_WIDTH:], final_norm[None], batch, seq)
    return out.reshape(batch, seq, D_MODEL)
```

```python
import functools

import jax
import jax.numpy as jnp
import numpy as np
from jax.experimental import pallas as pl
from jax.experimental.pallas import tpu as pltpu

D_MODEL = 1024
HEAD_DIM = 64
BLOCK = 128
A_HEADS = 8
A_KV_HEADS = 2
A_REP = A_HEADS // A_KV_HEADS
A_WIDTH = A_HEADS * HEAD_DIM
WINDOW = 128
B_HEADS = 8
B_NOPE = 64
B_ROPE = 32
B_V = 64
B_WIDTH = B_HEADS * B_V
Q_LORA = 384
KV_LORA = 256
ROPE_BASE = 10000.0
EPS = 1e-6

LANES = 128
HALF_ROPE = B_ROPE // 2
NEG = -1e30
LOG2E = float(np.log2(np.e))
QA_SCALE = HEAD_DIM ** -0.5 * LOG2E
QB_SCALE = (B_NOPE + B_ROPE) ** -0.5 * LOG2E
VMEM_LIMIT = 56 * 1024 * 1024

NOPE_A = 48

A_PAIR_COLS = np.concatenate(
    [np.arange(HEAD_DIM) + (g * A_REP + p) * HEAD_DIM for p in range(A_REP) for g in range(A_KV_HEADS)])


def _slab_cols(x1, nope, x2, zero):
    return np.concatenate([x1, nope[:NOPE_A], x2, nope[NOPE_A:],
                           np.full(LANES - B_NOPE - B_ROPE, zero)])


PER_Q = B_NOPE + B_ROPE
PER_KV = B_NOPE + B_V
UQ_SLAB_COLS = np.concatenate([
    _slab_cols(h * PER_Q + B_NOPE + np.arange(HALF_ROPE), h * PER_Q + np.arange(B_NOPE),
               h * PER_Q + B_NOPE + HALF_ROPE + np.arange(HALF_ROPE), B_HEADS * PER_Q)
    for h in range(B_HEADS)])
UK_SLAB_COLS = np.concatenate([
    _slab_cols(np.full(HALF_ROPE, B_HEADS * PER_KV), h * PER_KV + np.arange(B_NOPE),
               np.full(HALF_ROPE, B_HEADS * PER_KV), B_HEADS * PER_KV)
    for h in range(B_HEADS)])
UV_COLS = np.concatenate([h * PER_KV + B_NOPE + np.arange(B_V) for h in range(B_HEADS)])

C_QA = 0
C_KV = C_QA + A_WIDTH
C_GATE = C_KV + 2 * A_KV_HEADS * HEAD_DIM
C_CQ = C_GATE + A_WIDTH + B_WIDTH
C_CKV = C_CQ + Q_LORA
C_KR = C_CKV + KV_LORA
C_END = C_KR + LANES

TM_PROJ = 1024
ROWS_PROJ = 256
QB_WIN = 8
GROUP_WIN = 2
TQ_LAT = 512
CK_LAT = 256
BOUND_SLACK = 1.0 + 2.0 ** -10
L_FLOOR = 2.0 ** -80


def _rms(x, gain):
    return x * jax.lax.rsqrt(jnp.mean(x * x, axis=-1, keepdims=True) + EPS) * gain


def _emit_pipeline(n_groups, stages):
    state = [None] * n_groups
    for tick in range(n_groups + len(stages) - 1):
        for s in reversed(range(len(stages))):
            i = tick - s
            if 0 <= i < n_groups:
                state[i] = stages[s](i, state[i])


def _sq_norm(slab_bf16):
    v = slab_bf16.astype(jnp.float32)
    return jnp.sum(v * v, axis=-1, keepdims=True)


def _rope(slab, cos, sin):
    return slab * cos + pltpu.roll(slab, 64, 1) * sin


def _proj_kernel(x_ref, g_ref, win_ref, qg_ref, wuq_ref, kvg_ref, wk_ref, wvt_ref,
                 cq_ref, sq_ref, ck_ref, sk_ref,
                 qa_ref, kva_ref, gate_ref, qb_ref, kb_ref, qn_ref, kn_ref, vbt_ref):
    bf = jnp.bfloat16
    lane = jax.lax.broadcasted_iota(jnp.int32, (ROWS_PROJ, LANES), 1)
    rows = lambda i: pl.ds(i * ROWS_PROJ, ROWS_PROJ)

    def seg(h, lo, hi):
        return jnp.dot(h, win_ref[:, lo:hi], preferred_element_type=jnp.float32)

    def head_slabs(i, slab_of, out_ref, norm_ref):
        norms = jnp.zeros(lane.shape, jnp.float32)
        for hd in range(B_HEADS):
            slab = slab_of(slice(hd * LANES, (hd + 1) * LANES)).astype(bf)
            out_ref[hd, rows(i), :] = slab
            norms = jnp.where(lane == hd, _sq_norm(slab), norms)
        norm_ref[rows(i), :] = norms

    def s_norm(i, _):
        return _rms(x_ref[rows(i), :], g_ref[...]).astype(bf)

    def s_latent(i, h):
        cqn = _rms(seg(h, C_CQ, C_CKV), qg_ref[...]).astype(bf)
        ckvn = _rms(seg(h, C_CKV, C_KR), kvg_ref[...]).astype(bf)
        kr = _rope(seg(h, C_KR, C_END), ck_ref[rows(i), :], sk_ref[rows(i), :])
        return h, cqn, ckvn, kr

    def s_qa(i, st):
        qa_ref[rows(i), :] = (seg(st[0], C_QA, C_KV) * QA_SCALE).astype(bf)
        return st

    def s_q(i, st):
        q = jnp.dot(st[1], wuq_ref[...], preferred_element_type=jnp.float32)
        cq, sq = cq_ref[rows(i), :], sq_ref[rows(i), :]
        head_slabs(i, lambda sl: _rope(q[:, sl], cq, sq), qb_ref, qn_ref)
        return st

    def s_kva(i, st):
        kva_ref[rows(i), :] = seg(st[0], C_KV, C_GATE).astype(bf)
        return st

    def s_k(i, st):
        kn = jnp.dot(st[2], wk_ref[...], preferred_element_type=jnp.float32)
        head_slabs(i, lambda sl: kn[:, sl] + st[3], kb_ref, kn_ref)
        return st

    def s_v(i, st):
        vbt_ref[:, rows(i)] = jax.lax.dot_general(
            wvt_ref[...], st[2], (((1,), (1,)), ((), ())),
            preferred_element_type=jnp.float32).astype(bf)
        return st

    def s_gate(i, st):
        g = seg(st[0], C_GATE, C_CQ)
        gate_ref[rows(i), :] = (g / (1.0 + jnp.exp(-g))).astype(bf)
        return st

    _emit_pipeline(x_ref.shape[0] // ROWS_PROJ,
                   [s_norm, s_latent, s_qa, s_q, s_kva, s_k, s_v, s_gate])


def _input_proj(x2, norm_g, w_in_p, q_g, w_uq_p, kv_g, w_k_p, w_vt, cq, sq, ck, sk, seq):
    t = x2.shape[0]
    tm = TM_PROJ
    n_pos = seq // tm
    full = lambda a: pl.BlockSpec(a.shape, lambda i: (0,) * a.ndim)
    rows = lambda w: pl.BlockSpec((tm, w), lambda i: (i, 0))
    tab = pl.BlockSpec((tm, LANES), lambda i: (i % n_pos, 0))
    bf = jnp.bfloat16
    widths = (A_WIDTH, 2 * A_KV_HEADS * HEAD_DIM, A_WIDTH + B_WIDTH)
    heads = pl.BlockSpec((None, B_HEADS, tm, LANES), lambda i: (i // n_pos, 0, i % n_pos, 0))
    heads_shape = jax.ShapeDtypeStruct((t // seq, B_HEADS, seq, LANES), bf)
    return pl.pallas_call(
        _proj_kernel,
        grid=(t // tm,),
        in_specs=[rows(D_MODEL), full(norm_g), full(w_in_p), full(q_g), full(w_uq_p),
                  full(kv_g), full(w_k_p), full(w_vt), tab, tab, tab, tab],
        out_specs=[rows(w) for w in widths] + [heads, heads, rows(LANES), rows(LANES)]
        + [pl.BlockSpec((None, B_WIDTH, tm), lambda i: (i // n_pos, 0, i % n_pos))],
        out_shape=[jax.ShapeDtypeStruct((t, w), bf) for w in widths]
        + [heads_shape, heads_shape]
        + [jax.ShapeDtypeStruct((t, LANES), jnp.float32)] * 2
        + [jax.ShapeDtypeStruct((t // seq, B_WIDTH, seq), bf)],
        compiler_params=pltpu.CompilerParams(
            dimension_semantics=("arbitrary",), vmem_limit_bytes=VMEM_LIMIT),
        name="input_proj",
    )(x2, norm_g, w_in_p, q_g, w_uq_p, kv_g, w_k_p, w_vt, cq, sq, ck, sk)


class _WindowStages:
    def __init__(self, refs, step, n_blocks):
        (self.q_ref, self.kva_ref, self.gate_ref, self.bias_ref, self.sink_ref, self.yb_ref,
         self.x_ref, self.wa_ref, self.wb_ref, self.g_ref, self.o_ref, self.vt_ref,
         self.ya_ref) = refs
        self.step = step
        self.n_blocks = n_blocks
        self.first = jax.lax.broadcasted_iota(jnp.int32, (BLOCK, LANES), 1) < HEAD_DIM

    def _band(self, blk):
        n = self.step * QB_WIN + blk
        return (jnp.maximum(n - 1, 0), n, jnp.minimum(n + 1, self.n_blocks - 1))

    def _bias_idx(self, blk):
        if blk == 0:
            return jnp.where(self.step == 0, 0, 1)
        if blk == QB_WIN - 1:
            return jnp.where(self.step == self.n_blocks // QB_WIN - 1, 2, 1)
        return 1

    def scores(self, blk):
        rows = pl.ds(blk * BLOCK, BLOCK)
        kk = jnp.concatenate(
            [self.kva_ref[pl.ds(pl.multiple_of(b * BLOCK, BLOCK), BLOCK), :LANES]
             for b in self._band(blk)], axis=0)
        qp = [self.q_ref[rows, p * LANES:(p + 1) * LANES] for p in range(A_REP)]
        zero = jnp.zeros((), jnp.bfloat16)
        out = []
        for g in range(A_KV_HEADS):
            keep = self.first if g == 0 else jnp.logical_not(self.first)
            q4 = jnp.concatenate([jnp.where(keep, x, zero) for x in qp], axis=0)
            st = jax.lax.dot_general(kk, q4, (((1,), (1,)), ((), ())),
                                     preferred_element_type=jnp.float32)
            out.append(st + self.bias_ref[self._bias_idx(blk), g])
        return out

    def softmax(self, sts):
        out = []
        for g, st in enumerate(sts):
            sink = self.sink_ref[g:g + 1, :]
            m = jnp.maximum(jnp.max(st, axis=0, keepdims=True), sink)
            p = jnp.exp2(st - m)
            inv = 1.0 / (jnp.sum(p, axis=0, keepdims=True) + jnp.exp2(sink - m))
            out.append((p.astype(jnp.bfloat16), inv))
        return out

    def values(self, blk, probs):
        rows = pl.ds(blk * BLOCK, BLOCK)
        vts = [self.vt_ref[b] for b in self._band(blk)]
        outs = []
        for g, (pb, inv) in enumerate(probs):
            vt = jnp.concatenate([x[g * HEAD_DIM:(g + 1) * HEAD_DIM, :] for x in vts], axis=1)
            outs.append(jnp.dot(vt, pb, preferred_element_type=jnp.float32) * inv)
        for pr in range(A_REP):
            sl = slice(pr * LANES, (pr + 1) * LANES)
            o = jnp.concatenate([outs[0][:, sl], outs[1][:, sl]], axis=0).T
            self.ya_ref[rows, sl] = (o * self.gate_ref[rows, sl].astype(jnp.float32)
                                     ).astype(jnp.bfloat16)

    def out_proj(self, rows):
        d = jnp.dot(self.ya_ref[rows, :], self.wa_ref[...], preferred_element_type=jnp.float32)
        d = d + jnp.dot(self.yb_ref[rows, :], self.wb_ref[...], preferred_element_type=jnp.float32)
        return self.x_ref[rows, :] + d

    def norm_store(self, rows, y):
        self.o_ref[rows, :] = _rms(y, self.g_ref[...])


def _window_out_kernel(*refs, n_blocks):
    kva_ref, vt_ref = refs[1], refs[11]
    step = pl.program_id(1)

    @pl.when(step == 0)
    def _():
        for b in range(n_blocks):
            v = kva_ref[b * BLOCK:(b + 1) * BLOCK, LANES:].astype(jnp.float32)
            vt_ref[b] = v.T.astype(jnp.bfloat16)

    st = _WindowStages(refs, step, n_blocks)
    groups = [list(range(g, g + GROUP_WIN)) for g in range(0, QB_WIN, GROUP_WIN)]
    rows = [pl.ds(g[0] * BLOCK, len(g) * BLOCK) for g in groups]
    _emit_pipeline(len(groups), [
        lambda i, _: [st.scores(b) for b in groups[i]],
        lambda i, scores: [st.softmax(s) for s in scores],
        lambda i, probs: [st.values(b, p) for b, p in zip(groups[i], probs)],
        lambda i, _: st.out_proj(rows[i]),
        lambda i, y: st.norm_store(rows[i], y),
    ])


def _window_attn_out(qa, kva, gate, bias, sink_tab, yb, x2, w_a, w_b, final_g, batch, seq):
    nb = seq // BLOCK
    ns = nb // QB_WIN
    tq = QB_WIN * BLOCK
    rows = lambda w: pl.BlockSpec((tq, w), lambda b, n: (b * ns + n, 0))
    full = lambda a: pl.BlockSpec(a.shape, lambda b, n: (0,) * a.ndim)
    return pl.pallas_call(
        functools.partial(_window_out_kernel, n_blocks=nb),
        grid=(batch, ns),
        in_specs=[rows(A_WIDTH),
                  pl.BlockSpec((seq, 2 * A_KV_HEADS * HEAD_DIM), lambda b, n: (b, 0)),
                  rows(A_WIDTH), full(bias), full(sink_tab), rows(B_WIDTH), rows(D_MODEL),
                  full(w_a), full(w_b), full(final_g)],
        out_specs=rows(D_MODEL),
        out_shape=jax.ShapeDtypeStruct((batch * seq, D_MODEL), jnp.float32),
        scratch_shapes=[pltpu.VMEM((nb, LANES, BLOCK), jnp.bfloat16),
                        pltpu.VMEM((tq, A_WIDTH), jnp.bfloat16)],
        compiler_params=pltpu.CompilerParams(
            dimension_semantics=("arbitrary", "arbitrary"), vmem_limit_bytes=VMEM_LIMIT),
        name="window_attn_out",
    )(qa, kva, gate, bias, sink_tab, yb, x2, w_a, w_b, final_g)


def _latent_kernel(q_ref, k_ref, vt_ref, gate_ref, qn_ref, kn_ref, o_ref):
    n_chunks = k_ref.shape[1] // CK_LAT

    def scores(hd):
        return jax.lax.dot_general(k_ref[hd], q_ref[hd], (((1,), (1,)), ((), ())),
                                   preferred_element_type=jnp.float32)

    def weighted_values(hd, st, shift):
        l = acc = None
        for c in range(n_chunks):
            keys = slice(c * CK_LAT, (c + 1) * CK_LAT)
            p = jnp.exp2(st[keys] - shift)
            lc = jnp.sum(p, axis=0, keepdims=True)
            ac = jnp.dot(vt_ref[hd * B_V:(hd + 1) * B_V, keys], p.astype(jnp.bfloat16),
                         preferred_element_type=jnp.float32)
            l = lc if c == 0 else l + lc
            acc = ac if c == 0 else acc + ac
        return acc, l

    def store_pair(hd, outs):
        o = jnp.concatenate(outs, axis=0).T
        sl = slice((hd // 2) * LANES, (hd // 2 + 1) * LANES)
        o_ref[:, sl] = (o * gate_ref[:, sl].astype(jnp.float32)).astype(jnp.bfloat16)

    k_norm2 = jnp.max(kn_ref[...], axis=0, keepdims=True)
    q_norm2 = qn_ref[...].T
    outs, l_min = [], None
    for hd in range(B_HEADS):
        shift = jnp.sqrt(q_norm2[hd:hd + 1, :] * k_norm2[:, hd:hd + 1]) * BOUND_SLACK
        acc, l = weighted_values(hd, scores(hd), shift)
        l_min = l if hd == 0 else jnp.minimum(l_min, l)
        outs.append(acc * (1.0 / l))
        if hd % 2 == 1:
            store_pair(hd, outs)
            outs = []

    @pl.when(jnp.logical_not(jnp.min(l_min) >= L_FLOOR))
    def _():
        outs = []
        nxt = scores(0)
        for hd in range(B_HEADS):
            st = nxt
            if hd + 1 < B_HEADS:
                nxt = scores(hd + 1)
            acc, l = weighted_values(hd, st, jnp.max(st, axis=0, keepdims=True))
            outs.append(acc * (1.0 / l))
            if hd % 2 == 1:
                store_pair(hd, outs)
                outs = []


def _latent_attn(qb, kb, vbt, gate, qn, kn, batch, seq):
    tq = TQ_LAT
    nq = seq // tq
    return pl.pallas_call(
        _latent_kernel,
        grid=(batch, nq),
        in_specs=[pl.BlockSpec((None, B_HEADS, tq, LANES), lambda b, i: (b, 0, i, 0)),
                  pl.BlockSpec((None, B_HEADS, seq, LANES), lambda b, i: (b, 0, 0, 0)),
                  pl.BlockSpec((None, B_WIDTH, seq), lambda b, i: (b, 0, 0)),
                  pl.BlockSpec((tq, B_WIDTH), lambda b, i: (b * nq + i, 1)),
                  pl.BlockSpec((tq, LANES), lambda b, i: (b * nq + i, 0)),
                  pl.BlockSpec((seq, LANES), lambda b, i: (b, 0))],
        out_specs=pl.BlockSpec((tq, B_WIDTH), lambda b, i: (b * nq + i, 0)),
        out_shape=jax.ShapeDtypeStruct((batch * seq, B_WIDTH), jnp.bfloat16),
        compiler_params=pltpu.CompilerParams(
            dimension_semantics=("arbitrary", "arbitrary"), vmem_limit_bytes=VMEM_LIMIT),
        name="latent_attn",
    )(qb, kb, vbt, gate, qn, kn)


def _head_slab(x1, nope, x2):
    z = jnp.zeros((nope.shape[0], LANES - B_NOPE - B_ROPE), nope.dtype)
    return jnp.concatenate([x1, nope[:, :NOPE_A], x2, nope[:, NOPE_A:], z], axis=1)


def _select_cols(w, cols):
    onehot = np.zeros((w.shape[1], len(cols)), np.float32)
    valid = cols < w.shape[1]
    onehot[cols[valid], np.arange(len(cols))[valid]] = 1.0
    return jnp.dot(w, jnp.asarray(onehot, w.dtype), preferred_element_type=w.dtype)


def _pack_weights(w_in, w_uq, w_ukv):
    bf = jnp.bfloat16
    w_in, w_uq, w_ukv = w_in.astype(bf), w_uq.astype(bf), w_ukv.astype(bf)
    sizes = (A_WIDTH, A_KV_HEADS * HEAD_DIM, A_KV_HEADS * HEAD_DIM, A_WIDTH,
             Q_LORA, KV_LORA, B_ROPE, B_WIDTH)
    offs = np.cumsum((0,) + sizes)
    qa, ka, va, ga, cq, ckv, kr, gb = (w_in[:, offs[i]:offs[i + 1]] for i in range(8))
    zr = jnp.zeros((w_in.shape[0], B_NOPE), w_in.dtype)
    kr_slab = _head_slab(kr[:, :HALF_ROPE], zr, kr[:, HALF_ROPE:])
    w_in_p = jnp.concatenate(
        [_select_cols(qa, A_PAIR_COLS), ka, va, _select_cols(ga, A_PAIR_COLS), gb, cq, ckv, kr_slab], axis=1)

    return (w_in_p, _select_cols(w_uq, UQ_SLAB_COLS), _select_cols(w_ukv, UK_SLAB_COLS),
            _select_cols(w_ukv, UV_COLS).T)


def _rope_tables(seq):
    pos = np.arange(seq, dtype=np.float32)
    inv_freq = (np.float32(ROPE_BASE) ** (-np.arange(0, B_ROPE, 2, dtype=np.float32) / B_ROPE)
                ).astype(np.float32)
    ang = pos[:, None] * inv_freq[None, :]
    cos, sin = np.cos(ang), np.sin(ang)
    one = np.ones((seq, B_NOPE), np.float32)
    zero = np.zeros((seq, B_NOPE), np.float32)
    pad = np.zeros((seq, LANES - B_NOPE - B_ROPE), np.float32)
    c = np.concatenate([cos, one[:, :NOPE_A], cos, one[:, NOPE_A:], pad], axis=1)
    s = np.concatenate([-sin, zero[:, :NOPE_A], sin, zero[:, NOPE_A:], pad], axis=1)
    qs = np.float32(QB_SCALE)
    return tuple(jnp.asarray(t.astype(np.float32)) for t in (c * qs, s * qs, c, s))


def _window_bias():
    kj = np.arange(3 * BLOCK)[:, None]
    qi = np.arange(BLOCK)[None, :]
    dist = np.abs(qi - kj + BLOCK).astype(np.float32)
    slopes = np.exp2(-8.0 * np.arange(1, A_HEADS + 1, dtype=np.float32) / A_HEADS)
    bias = -slopes[:, None, None] * dist[None] * LOG2E
    bias = np.where((dist <= WINDOW)[None], bias, NEG).astype(np.float32)
    bias = bias.reshape(A_KV_HEADS, A_REP, 3 * BLOCK, BLOCK).transpose(0, 2, 1, 3)
    bias = bias.reshape(A_KV_HEADS, 3 * BLOCK, A_REP * BLOCK)
    first, last = bias.copy(), bias.copy()
    first[:, :BLOCK] = NEG
    last[:, 2 * BLOCK:] = NEG
    return jnp.asarray(np.stack([first, bias, last]))


def kernel(x, norm_mix, w_in, attn_sink, q_a_norm, w_uq, kv_a_norm, w_ukv, w_out, final_norm):
    batch, seq, _ = x.shape
    assert norm_mix.shape[0] == 1, "single-layer block: the final norm is fused into the layer"
    assert seq % TM_PROJ == 0 and seq % TQ_LAT == 0 and seq % CK_LAT == 0
    assert seq % (QB_WIN * BLOCK) == 0 and QB_WIN % GROUP_WIN == 0 and TM_PROJ % ROWS_PROJ == 0
    cq, sq, ck, sk = _rope_tables(seq)
    bias = _window_bias()
    x2 = x.reshape(batch * seq, D_MODEL)
    w_in_p, w_uq_p, w_k_p, w_vt = _pack_weights(w_in[0], w_uq[0], w_ukv[0])
    qa, kva, gate, qb, kb, qn, kn, vbt = _input_proj(
        x2, norm_mix, w_in_p, q_a_norm, w_uq_p, kv_a_norm, w_k_p, w_vt, cq, sq, ck, sk, seq)
    sink_tab = jnp.repeat(attn_sink[0].astype(jnp.float32) * LOG2E, BLOCK).reshape(A_KV_HEADS, A_REP * BLOCK)
    yb = _latent_attn(qb, kb, vbt, gate, qn, kn, batch, seq)
    w_o = w_out[0].astype(jnp.bfloat16)
    out = _window_attn_out(qa, kva, gate, bias, sink_tab, yb, x2,
                           _select_cols(w_o[:A_WIDTH].T, A_PAIR_COLS).T, w_o[A_WIDTH:], final_norm[None], batch, seq)
    return out.reshape(batch, seq, D_MODEL)
```

```python
import functools

import jax
import jax.numpy as jnp
import numpy as np
from jax.experimental import pallas as pl
from jax.experimental.pallas import tpu as pltpu

D_MODEL = 1024
HEAD_DIM = 64
BLOCK = 128
A_HEADS = 8
A_KV_HEADS = 2
A_REP = A_HEADS // A_KV_HEADS
A_WIDTH = A_HEADS * HEAD_DIM
WINDOW = 128
B_HEADS = 8
B_NOPE = 64
B_ROPE = 32
B_V = 64
B_WIDTH = B_HEADS * B_V
Q_LORA = 384
KV_LORA = 256
ROPE_BASE = 10000.0
EPS = 1e-6

LANES = 128
HALF_ROPE = B_ROPE // 2
NEG = -1e30
LOG2E = float(np.log2(np.e))
QA_SCALE = HEAD_DIM ** -0.5 * LOG2E
QB_SCALE = (B_NOPE + B_ROPE) ** -0.5 * LOG2E
VMEM_LIMIT = 56 * 1024 * 1024

NOPE_A = 48

A_PAIR_COLS = np.concatenate(
    [np.arange(HEAD_DIM) + (g * A_REP + p) * HEAD_DIM for p in range(A_REP) for g in range(A_KV_HEADS)])


def _slab_cols(x1, nope, x2, zero):
    return np.concatenate([x1, nope[:NOPE_A], x2, nope[NOPE_A:],
                           np.full(LANES - B_NOPE - B_ROPE, zero)])


PER_Q = B_NOPE + B_ROPE
PER_KV = B_NOPE + B_V
UQ_SLAB_COLS = np.concatenate([
    _slab_cols(h * PER_Q + B_NOPE + np.arange(HALF_ROPE), h * PER_Q + np.arange(B_NOPE),
               h * PER_Q + B_NOPE + HALF_ROPE + np.arange(HALF_ROPE), B_HEADS * PER_Q)
    for h in range(B_HEADS)])
UK_SLAB_COLS = np.concatenate([
    _slab_cols(np.full(HALF_ROPE, B_HEADS * PER_KV), h * PER_KV + np.arange(B_NOPE),
               np.full(HALF_ROPE, B_HEADS * PER_KV), B_HEADS * PER_KV)
    for h in range(B_HEADS)])
UV_COLS = np.concatenate([h * PER_KV + B_NOPE + np.arange(B_V) for h in range(B_HEADS)])

C_KV = A_WIDTH
C_GA = C_KV + 2 * A_KV_HEADS * HEAD_DIM
C_CQ = C_GA + A_WIDTH
C_CKV = C_CQ + Q_LORA
C_KR = C_CKV + KV_LORA

TM_PROJ = 1024
ROWS_PROJ = 256
QB_WIN = 8
GROUP_WIN = 2
TQ_LAT = 512
CK_LAT = 256
BOUND_SLACK = 1.0 + 2.0 ** -10
L_FLOOR = 2.0 ** -80


def _rms(x, gain):
    return x * jax.lax.rsqrt(jnp.mean(x * x, axis=-1, keepdims=True) + EPS) * gain


def _emit_pipeline(n_groups, stages):
    state = [None] * n_groups
    for tick in range(n_groups + len(stages) - 1):
        for s in reversed(range(len(stages))):
            i = tick - s
            if 0 <= i < n_groups:
                state[i] = stages[s](i, state[i])


def _sq_norm(slab_bf16):
    v = slab_bf16.astype(jnp.float32)
    return jnp.sum(v * v, axis=-1, keepdims=True)


def _rope(slab, cos, sin):
    return slab * cos + pltpu.roll(slab, 64, 1) * sin


def _proj_kernel(x_ref, g_ref, win_ref, wqa_ref, wga_ref, wgb_ref, wkr_ref, qg_ref, wuq_ref, kvg_ref, wk_ref, wvt_ref,
                 cq_ref, sq_ref, ck_ref, sk_ref,
                 qa_ref, kva_ref, gate_ref, qb_ref, kb_ref, qn_ref, kn_ref, vbt_ref):
    bf = jnp.bfloat16
    lane = jax.lax.broadcasted_iota(jnp.int32, (ROWS_PROJ, LANES), 1)
    rows = lambda i: pl.ds(i * ROWS_PROJ, ROWS_PROJ)

    def seg(h, w):
        return jnp.dot(h, w, preferred_element_type=jnp.float32)

    def head_slabs(i, slab_of, out_ref, norm_ref):
        norms = jnp.zeros(lane.shape, jnp.float32)
        for hd in range(B_HEADS):
            slab = slab_of(slice(hd * LANES, (hd + 1) * LANES)).astype(bf)
            out_ref[hd, rows(i), :] = slab
            norms = jnp.where(lane == hd, _sq_norm(slab), norms)
        norm_ref[rows(i), :] = norms

    def s_norm(i, _):
        return _rms(x_ref[rows(i), :], g_ref[...]).astype(bf)

    def s_latent(i, h):
        cqn = _rms(seg(h, win_ref[:, C_CQ:C_CKV]), qg_ref[...]).astype(bf)
        ckvn = _rms(seg(h, win_ref[:, C_CKV:C_KR]), kvg_ref[...]).astype(bf)
        kr = _rope(seg(h, wkr_ref[...]), ck_ref[rows(i), :], sk_ref[rows(i), :])
        return h, cqn, ckvn, kr

    def s_qa(i, st):
        qa_ref[rows(i), :] = (seg(st[0], wqa_ref[...]) * QA_SCALE).astype(bf)
        return st

    def s_q(i, st):
        q = jnp.dot(st[1], wuq_ref[...], preferred_element_type=jnp.float32)
        cq, sq = cq_ref[rows(i), :], sq_ref[rows(i), :]
        head_slabs(i, lambda sl: _rope(q[:, sl], cq, sq), qb_ref, qn_ref)
        return st

    def s_kva(i, st):
        kva_ref[rows(i), :] = seg(st[0], win_ref[:, C_KV:C_GA]).astype(bf)
        return st

    def s_k(i, st):
        kn = jnp.dot(st[2], wk_ref[...], preferred_element_type=jnp.float32)
        head_slabs(i, lambda sl: kn[:, sl] + st[3], kb_ref, kn_ref)
        return st

    def s_v(i, st):
        vbt_ref[:, rows(i)] = jax.lax.dot_general(
            wvt_ref[...], st[2], (((1,), (1,)), ((), ())),
            preferred_element_type=jnp.float32).astype(bf)
        return st

    def s_gate(i, st):
        for cols, w_ref in ((slice(0, A_WIDTH), wga_ref), (slice(A_WIDTH, A_WIDTH + B_WIDTH), wgb_ref)):
            g = seg(st[0], w_ref[...])
            gate_ref[rows(i), cols] = (g / (1.0 + jnp.exp(-g))).astype(bf)
        return st

    _emit_pipeline(x_ref.shape[0] // ROWS_PROJ,
                   [s_norm, s_latent, s_qa, s_q, s_kva, s_k, s_v, s_gate])


def _input_proj(x2, norm_g, w_in_parts, q_g, w_uq_p, kv_g, w_k_p, w_vt, cq, sq, ck, sk, seq):
    t = x2.shape[0]
    tm = TM_PROJ
    n_pos = seq // tm
    full = lambda a: pl.BlockSpec(a.shape, lambda i: (0,) * a.ndim)
    rows = lambda w: pl.BlockSpec((tm, w), lambda i: (i, 0))
    tab = pl.BlockSpec((tm, LANES), lambda i: (i % n_pos, 0))
    bf = jnp.bfloat16
    widths = (A_WIDTH, 2 * A_KV_HEADS * HEAD_DIM, A_WIDTH + B_WIDTH)
    heads = pl.BlockSpec((None, B_HEADS, tm, LANES), lambda i: (i // n_pos, 0, i % n_pos, 0))
    heads_shape = jax.ShapeDtypeStruct((t // seq, B_HEADS, seq, LANES), bf)
    return pl.pallas_call(
        _proj_kernel,
        grid=(t // tm,),
        in_specs=[rows(D_MODEL), full(norm_g)] + [full(w) for w in w_in_parts]
        + [full(q_g), full(w_uq_p), full(kv_g), full(w_k_p), full(w_vt), tab, tab, tab, tab],
        out_specs=[rows(w) for w in widths] + [heads, heads, rows(LANES), rows(LANES)]
        + [pl.BlockSpec((None, B_WIDTH, tm), lambda i: (i // n_pos, 0, i % n_pos))],
        out_shape=[jax.ShapeDtypeStruct((t, w), bf) for w in widths]
        + [heads_shape, heads_shape]
        + [jax.ShapeDtypeStruct((t, LANES), jnp.float32)] * 2
        + [jax.ShapeDtypeStruct((t // seq, B_WIDTH, seq), bf)],
        compiler_params=pltpu.CompilerParams(
            dimension_semantics=("arbitrary",), vmem_limit_bytes=VMEM_LIMIT),
        name="input_proj",
    )(x2, norm_g, *w_in_parts, q_g, w_uq_p, kv_g, w_k_p, w_vt, cq, sq, ck, sk)


class _WindowStages:
    def __init__(self, refs, step, n_blocks):
        (self.q_ref, self.kva_ref, self.gate_ref, self.bias_ref, self.sink_ref, self.yb_ref,
         self.x_ref, self.wa_ref, self.wb_ref, self.g_ref, self.o_ref, self.vt_ref,
         self.ya_ref) = refs
        self.step = step
        self.n_blocks = n_blocks
        self.first = jax.lax.broadcasted_iota(jnp.int32, (BLOCK, LANES), 1) < HEAD_DIM

    def _band(self, blk):
        n = self.step * QB_WIN + blk
        return (jnp.maximum(n - 1, 0), n, jnp.minimum(n + 1, self.n_blocks - 1))

    def _bias_idx(self, blk):
        if blk == 0:
            return jnp.where(self.step == 0, 0, 1)
        if blk == QB_WIN - 1:
            return jnp.where(self.step == self.n_blocks // QB_WIN - 1, 2, 1)
        return 1

    def scores(self, blk):
        rows = pl.ds(blk * BLOCK, BLOCK)
        kk = jnp.concatenate(
            [self.kva_ref[pl.ds(pl.multiple_of(b * BLOCK, BLOCK), BLOCK), :LANES]
             for b in self._band(blk)], axis=0)
        qp = [self.q_ref[rows, p * LANES:(p + 1) * LANES] for p in range(A_REP)]
        zero = jnp.zeros((), jnp.bfloat16)
        out = []
        for g in range(A_KV_HEADS):
            keep = self.first if g == 0 else jnp.logical_not(self.first)
            q4 = jnp.concatenate([jnp.where(keep, x, zero) for x in qp], axis=0)
            st = jax.lax.dot_general(kk, q4, (((1,), (1,)), ((), ())),
                                     preferred_element_type=jnp.float32)
            out.append(st + self.bias_ref[self._bias_idx(blk), g])
        return out

    def softmax(self, sts):
        out = []
        for g, st in enumerate(sts):
            sink = self.sink_ref[g:g + 1, :]
            m = jnp.maximum(jnp.max(st, axis=0, keepdims=True), sink)
            p = jnp.exp2(st - m)
            inv = 1.0 / (jnp.sum(p, axis=0, keepdims=True) + jnp.exp2(sink - m))
            out.append((p.astype(jnp.bfloat16), inv))
        return out

    def values(self, blk, probs):
        rows = pl.ds(blk * BLOCK, BLOCK)
        vts = [self.vt_ref[b] for b in self._band(blk)]
        outs = []
        for g, (pb, inv) in enumerate(probs):
            vt = jnp.concatenate([x[g * HEAD_DIM:(g + 1) * HEAD_DIM, :] for x in vts], axis=1)
            outs.append(jnp.dot(vt, pb, preferred_element_type=jnp.float32) * inv)
        for pr in range(A_REP):
            sl = slice(pr * LANES, (pr + 1) * LANES)
            o = jnp.concatenate([outs[0][:, sl], outs[1][:, sl]], axis=0).T
            self.ya_ref[rows, sl] = (o * self.gate_ref[rows, sl].astype(jnp.float32)
                                     ).astype(jnp.bfloat16)

    def out_proj(self, rows):
        d = jnp.dot(self.ya_ref[rows, :], self.wa_ref[...], preferred_element_type=jnp.float32)
        d = d + jnp.dot(self.yb_ref[rows, :], self.wb_ref[...], preferred_element_type=jnp.float32)
        return self.x_ref[rows, :] + d

    def norm_store(self, rows, y):
        self.o_ref[rows, :] = _rms(y, self.g_ref[...])


def _window_out_kernel(*refs, n_blocks):
    kva_ref, vt_ref = refs[1], refs[11]
    step = pl.program_id(1)

    @pl.when(step == 0)
    def _():
        for b in range(n_blocks):
            v = kva_ref[b * BLOCK:(b + 1) * BLOCK, LANES:].astype(jnp.float32)
            vt_ref[b] = v.T.astype(jnp.bfloat16)

    st = _WindowStages(refs, step, n_blocks)
    groups = [list(range(g, g + GROUP_WIN)) for g in range(0, QB_WIN, GROUP_WIN)]
    rows = [pl.ds(g[0] * BLOCK, len(g) * BLOCK) for g in groups]
    _emit_pipeline(len(groups), [
        lambda i, _: [st.scores(b) for b in groups[i]],
        lambda i, scores: [st.softmax(s) for s in scores],
        lambda i, probs: [st.values(b, p) for b, p in zip(groups[i], probs)],
        lambda i, _: st.out_proj(rows[i]),
        lambda i, y: st.norm_store(rows[i], y),
    ])


def _window_attn_out(qa, kva, gate, bias, sink_tab, yb, x2, w_a, w_b, final_g, batch, seq):
    nb = seq // BLOCK
    ns = nb // QB_WIN
    tq = QB_WIN * BLOCK
    rows = lambda w: pl.BlockSpec((tq, w), lambda b, n: (b * ns + n, 0))
    full = lambda a: pl.BlockSpec(a.shape, lambda b, n: (0,) * a.ndim)
    return pl.pallas_call(
        functools.partial(_window_out_kernel, n_blocks=nb),
        grid=(batch, ns),
        in_specs=[rows(A_WIDTH),
                  pl.BlockSpec((seq, 2 * A_KV_HEADS * HEAD_DIM), lambda b, n: (b, 0)),
                  rows(A_WIDTH), full(bias), full(sink_tab), rows(B_WIDTH), rows(D_MODEL),
                  full(w_a), full(w_b), full(final_g)],
        out_specs=rows(D_MODEL),
        out_shape=jax.ShapeDtypeStruct((batch * seq, D_MODEL), jnp.float32),
        scratch_shapes=[pltpu.VMEM((nb, LANES, BLOCK), jnp.bfloat16),
                        pltpu.VMEM((tq, A_WIDTH), jnp.bfloat16)],
        compiler_params=pltpu.CompilerParams(
            dimension_semantics=("arbitrary", "arbitrary"), vmem_limit_bytes=VMEM_LIMIT),
        name="window_attn_out",
    )(qa, kva, gate, bias, sink_tab, yb, x2, w_a, w_b, final_g)


def _latent_kernel(q_ref, k_ref, vt_ref, gate_ref, qn_ref, kn_ref, o_ref):
    n_chunks = k_ref.shape[1] // CK_LAT

    def scores(hd):
        return jax.lax.dot_general(k_ref[hd], q_ref[hd], (((1,), (1,)), ((), ())),
                                   preferred_element_type=jnp.float32)

    def weighted_values(hd, st, shift):
        l = acc = None
        for c in range(n_chunks):
            keys = slice(c * CK_LAT, (c + 1) * CK_LAT)
            p = jnp.exp2(st[keys] - shift)
            lc = jnp.sum(p, axis=0, keepdims=True)
            ac = jnp.dot(vt_ref[hd * B_V:(hd + 1) * B_V, keys], p.astype(jnp.bfloat16),
                         preferred_element_type=jnp.float32)
            l = lc if c == 0 else l + lc
            acc = ac if c == 0 else acc + ac
        return acc, l

    def store_pair(hd, outs):
        o = jnp.concatenate(outs, axis=0).T
        sl = slice((hd // 2) * LANES, (hd // 2 + 1) * LANES)
        o_ref[:, sl] = (o * gate_ref[:, sl].astype(jnp.float32)).astype(jnp.bfloat16)

    k_norm2 = jnp.max(kn_ref[...], axis=0, keepdims=True)
    q_norm2 = qn_ref[...].T
    outs, l_min = [], None
    for hd in range(B_HEADS):
        shift = jnp.sqrt(q_norm2[hd:hd + 1, :] * k_norm2[:, hd:hd + 1]) * BOUND_SLACK
        acc, l = weighted_values(hd, scores(hd), shift)
        l_min = l if hd == 0 else jnp.minimum(l_min, l)
        outs.append(acc * (1.0 / l))
        if hd % 2 == 1:
            store_pair(hd, outs)
            outs = []

    @pl.when(jnp.logical_not(jnp.min(l_min) >= L_FLOOR))
    def _():
        outs = []
        nxt = scores(0)
        for hd in range(B_HEADS):
            st = nxt
            if hd + 1 < B_HEADS:
                nxt = scores(hd + 1)
            acc, l = weighted_values(hd, st, jnp.max(st, axis=0, keepdims=True))
            outs.append(acc * (1.0 / l))
            if hd % 2 == 1:
                store_pair(hd, outs)
                outs = []


def _latent_attn(qb, kb, vbt, gate, qn, kn, batch, seq):
    tq = TQ_LAT
    nq = seq // tq
    return pl.pallas_call(
        _latent_kernel,
        grid=(batch, nq),
        in_specs=[pl.BlockSpec((None, B_HEADS, tq, LANES), lambda b, i: (b, 0, i, 0)),
                  pl.BlockSpec((None, B_HEADS, seq, LANES), lambda b, i: (b, 0, 0, 0)),
                  pl.BlockSpec((None, B_WIDTH, seq), lambda b, i: (b, 0, 0)),
                  pl.BlockSpec((tq, B_WIDTH), lambda b, i: (b * nq + i, 1)),
                  pl.BlockSpec((tq, LANES), lambda b, i: (b * nq + i, 0)),
                  pl.BlockSpec((seq, LANES), lambda b, i: (b, 0))],
        out_specs=pl.BlockSpec((tq, B_WIDTH), lambda b, i: (b * nq + i, 0)),
        out_shape=jax.ShapeDtypeStruct((batch * seq, B_WIDTH), jnp.bfloat16),
        compiler_params=pltpu.CompilerParams(
            dimension_semantics=("arbitrary", "arbitrary"), vmem_limit_bytes=VMEM_LIMIT),
        name="latent_attn",
    )(qb, kb, vbt, gate, qn, kn)


def _head_slab(x1, nope, x2):
    z = jnp.zeros((nope.shape[0], LANES - B_NOPE - B_ROPE), nope.dtype)
    return jnp.concatenate([x1, nope[:, :NOPE_A], x2, nope[:, NOPE_A:], z], axis=1)


def _select_cols(w, cols):
    onehot = np.zeros((w.shape[1], len(cols)), np.float32)
    valid = cols < w.shape[1]
    onehot[cols[valid], np.arange(len(cols))[valid]] = 1.0
    return jnp.dot(w, jnp.asarray(onehot, w.dtype), preferred_element_type=w.dtype)


def _pack_weights(w_in, w_uq, w_ukv):
    bf = jnp.bfloat16
    w_in, w_uq, w_ukv = w_in.astype(bf), w_uq.astype(bf), w_ukv.astype(bf)
    sizes = (A_WIDTH, A_KV_HEADS * HEAD_DIM, A_KV_HEADS * HEAD_DIM, A_WIDTH,
             Q_LORA, KV_LORA, B_ROPE, B_WIDTH)
    offs = np.cumsum((0,) + sizes)
    qa, ka, va, ga, cq, ckv, kr, gb = (w_in[:, offs[i]:offs[i + 1]] for i in range(8))
    zr = jnp.zeros((w_in.shape[0], B_NOPE), w_in.dtype)
    kr_slab = _head_slab(kr[:, :HALF_ROPE], zr, kr[:, HALF_ROPE:])
    w_in_p = (w_in, _select_cols(qa, A_PAIR_COLS), _select_cols(ga, A_PAIR_COLS), gb, kr_slab)

    return (w_in_p, _select_cols(w_uq, UQ_SLAB_COLS), _select_cols(w_ukv, UK_SLAB_COLS),
            _select_cols(w_ukv, UV_COLS).T)


def _rope_tables(seq):
    pos = np.arange(seq, dtype=np.float32)
    inv_freq = (np.float32(ROPE_BASE) ** (-np.arange(0, B_ROPE, 2, dtype=np.float32) / B_ROPE)
                ).astype(np.float32)
    ang = pos[:, None] * inv_freq[None, :]
    cos, sin = np.cos(ang), np.sin(ang)
    one = np.ones((seq, B_NOPE), np.float32)
    zero = np.zeros((seq, B_NOPE), np.float32)
    pad = np.zeros((seq, LANES - B_NOPE - B_ROPE), np.float32)
    c = np.concatenate([cos, one[:, :NOPE_A], cos, one[:, NOPE_A:], pad], axis=1)
    s = np.concatenate([-sin, zero[:, :NOPE_A], sin, zero[:, NOPE_A:], pad], axis=1)
    qs = np.float32(QB_SCALE)
    return tuple(jnp.asarray(t.astype(np.float32)) for t in (c * qs, s * qs, c, s))


def _window_bias():
    kj = np.arange(3 * BLOCK)[:, None]
    qi = np.arange(BLOCK)[None, :]
    dist = np.abs(qi - kj + BLOCK).astype(np.float32)
    slopes = np.exp2(-8.0 * np.arange(1, A_HEADS + 1, dtype=np.float32) / A_HEADS)
    bias = -slopes[:, None, None] * dist[None] * LOG2E
    bias = np.where((dist <= WINDOW)[None], bias, NEG).astype(np.float32)
    bias = bias.reshape(A_KV_HEADS, A_REP, 3 * BLOCK, BLOCK).transpose(0, 2, 1, 3)
    bias = bias.reshape(A_KV_HEADS, 3 * BLOCK, A_REP * BLOCK)
    first, last = bias.copy(), bias.copy()
    first[:, :BLOCK] = NEG
    last[:, 2 * BLOCK:] = NEG
    return jnp.asarray(np.stack([first, bias, last]))


def kernel(x, norm_mix, w_in, attn_sink, q_a_norm, w_uq, kv_a_norm, w_ukv, w_out, final_norm):
    batch, seq, _ = x.shape
    assert norm_mix.shape[0] == 1, "single-layer block: the final norm is fused into the layer"
    assert seq % TM_PROJ == 0 and seq % TQ_LAT == 0 and seq % CK_LAT == 0
    assert seq % (QB_WIN * BLOCK) == 0 and QB_WIN % GROUP_WIN == 0 and TM_PROJ % ROWS_PROJ == 0
    cq, sq, ck, sk = _rope_tables(seq)
    bias = _window_bias()
    x2 = x.reshape(batch * seq, D_MODEL)
    w_in_parts, w_uq_p, w_k_p, w_vt = _pack_weights(w_in[0], w_uq[0], w_ukv[0])
    qa, kva, gate, qb, kb, qn, kn, vbt = _input_proj(
        x2, norm_mix, w_in_parts, q_a_norm, w_uq_p, kv_a_norm, w_k_p, w_vt, cq, sq, ck, sk, seq)
    sink_tab = jnp.repeat(attn_sink[0].astype(jnp.float32) * LOG2E, BLOCK).reshape(A_KV_HEADS, A_REP * BLOCK)
    yb = _latent_attn(qb, kb, vbt, gate, qn, kn, batch, seq)
    w_o = w_out[0].astype(jnp.bfloat16)
    out = _window_attn_out(qa, kva, gate, bias, sink_tab, yb, x2,
                           _select_cols(w_o[:A_WIDTH].T, A_PAIR_COLS).T, w_o[A_WIDTH:], final_norm[None], batch, seq)
    return out.reshape(batch, seq, D_MODEL)
```
